```python
import jax, jax.numpy as jnp
from jax import lax
import numpy as np

D_MODEL = 1024
BATCH = 8
SEQ = 2048
DEPTH = 4

N_A = DEPTH // 2
N_B = DEPTH - N_A
N_HEADS = 16
HEAD_DIM = D_MODEL // N_HEADS
D_FF = 2816
CONV_WIDTH = 31
Q_BLOCK = 128
RMS_EPS = 1e-6
LN_EPS = 1e-5
HALF_STEP = 0.5

kernel_name = "conformer_conv_stickbreaking_yoco"


def rms_norm(x, g):
    xf = x.astype(jnp.float32)
    y = xf * lax.rsqrt(jnp.mean(xf * xf, axis=-1, keepdims=True) + RMS_EPS)
    return (y * g.astype(jnp.float32)).astype(x.dtype)


def layer_norm(x, g, b):
    xf = x.astype(jnp.float32)
    mu = jnp.mean(xf, axis=-1, keepdims=True)
    xc = xf - mu
    var = jnp.mean(xc * xc, axis=-1, keepdims=True)
    y = xc * lax.rsqrt(var + LN_EPS) * g.astype(jnp.float32) + b.astype(jnp.float32)
    return y.astype(x.dtype)


def swiglu_ffn(x, w_gate, w_up, w_down):
    return (jax.nn.silu(x @ w_gate) * (x @ w_up)) @ w_down


def conformer_conv(x, w_pw1, b_pw1, w_dw, b_dw, ln_g, ln_b, w_pw2, b_pw2):
    h = x @ w_pw1 + b_pw1
    a, gate = jnp.split(h, 2, axis=-1)
    h = a * jax.nn.sigmoid(gate)
    h = lax.conv_general_dilated(
        h, w_dw[:, None, :].astype(h.dtype),
        window_strides=(1,), padding=[(CONV_WIDTH - 1, 0)],
        dimension_numbers=('NWC', 'WIO', 'NWC'),
        feature_group_count=D_MODEL) + b_dw
    h = jax.nn.silu(layer_norm(h, ln_g, ln_b))
    return h @ w_pw2 + b_pw2


def split_heads(t):
    b, s, _ = t.shape
    return t.reshape(b, s, N_HEADS, HEAD_DIM).transpose(0, 2, 1, 3)


def stick_breaking_attention(q, k, v):
    s_len = q.shape[2]
    scale = HEAD_DIM ** -0.5
    outs = []
    for blk in range(s_len // Q_BLOCK):
        start = blk * Q_BLOCK
        end = start + Q_BLOCK
        qb = q[:, :, start:end]
        kb = k[:, :, :end]
        vb = v[:, :, :end]
        z = jnp.einsum('bhqd,bhkd->bhqk', qb, kb, preferred_element_type=jnp.float32) * scale
        t_pos = jnp.arange(start, end)[:, None]
        s_pos = jnp.arange(end)[None, :]
        causal = s_pos < t_pos
        log_beta = jax.nn.log_sigmoid(z)
        log_one_minus = jnp.where(causal, jax.nn.log_sigmoid(-z), 0.0)
        log_remain = lax.cumsum(log_one_minus, axis=3, reverse=True) - log_one_minus
        weights = jnp.where(causal, jnp.exp(log_beta + log_remain), 0.0)
        outs.append(jnp.einsum('bhqk,bhkd->bhqd', weights.astype(vb.dtype), vb))
    return jnp.concatenate(outs, axis=2)


def setup_inputs(seed: int = 0) -> dict:
    key = jax.random.key(seed)
    ks = jax.random.split(key, 32)
    f32 = jnp.float32

    def w(k, shape, fan_in):
        return jax.random.normal(k, shape, f32) * (fan_in ** -0.5)

    def gain(k, shape):
        return 1.0 + 0.02 * jax.random.normal(k, shape, f32)

    def bias(k, shape):
        return 0.01 * jax.random.normal(k, shape, f32)

    D, F = D_MODEL, D_FF
    return {
        "x": jax.random.normal(ks[0], (BATCH, SEQ, D), f32),
        "ffn1_norm": gain(ks[1], (DEPTH, D)),
        "ffn1_w_gate": w(ks[2], (DEPTH, D, F), D),
        "ffn1_w_up": w(ks[3], (DEPTH, D, F), D),
        "ffn1_w_down": w(ks[4], (DEPTH, F, D), F),
        "mix_norm": gain(ks[5], (DEPTH, D)),
        "ffn2_norm": gain(ks[6], (DEPTH, D)),
        "ffn2_w_gate": w(ks[7], (DEPTH, D, F), D),
        "ffn2_w_up": w(ks[8], (DEPTH, D, F), D),
        "ffn2_w_down": w(ks[9], (DEPTH, F, D), F),
        "conv_w_pw1": w(ks[10], (N_A, D, 2 * D), D),
        "conv_b_pw1": bias(ks[11], (N_A, 2 * D)),
        "conv_w_dw": w(ks[12], (N_A, CONV_WIDTH, D), CONV_WIDTH),
        "conv_b_dw": bias(ks[13], (N_A, D)),
        "conv_ln_g": gain(ks[14], (N_A, D)),
        "conv_ln_b": bias(ks[15], (N_A, D)),
        "conv_w_pw2": w(ks[16], (N_A, D, D), D),
        "conv_b_pw2": bias(ks[17], (N_A, D)),
        "kv_norm": gain(ks[18], (D,)),
        "w_kv": w(ks[19], (D, 2 * D), D),
        "attn_w_q": w(ks[20], (N_B, D, D), D),
        "attn_w_o": w(ks[21], (N_B, D, D), D),
        "final_norm": gain(ks[22], (D,)),
    }


def reference(x, ffn1_norm, ffn1_w_gate, ffn1_w_up, ffn1_w_down, mix_norm,
              ffn2_norm, ffn2_w_gate, ffn2_w_up, ffn2_w_down,
              conv_w_pw1, conv_b_pw1, conv_w_dw, conv_b_dw, conv_ln_g, conv_ln_b,
              conv_w_pw2, conv_b_pw2, kv_norm, w_kv, attn_w_q, attn_w_o, final_norm):
    b, s, d = x.shape
    h = x
    k_shared = None
    v_shared = None
    for layer in range(DEPTH):
        h = h + HALF_STEP * swiglu_ffn(rms_norm(h, ffn1_norm[layer]),
                                       ffn1_w_gate[layer], ffn1_w_up[layer], ffn1_w_down[layer])
        u = rms_norm(h, mix_norm[layer])
        if layer < N_A:
            i = layer
            h = h + conformer_conv(u, conv_w_pw1[i], conv_b_pw1[i], conv_w_dw[i], conv_b_dw[i],
                                   conv_ln_g[i], conv_ln_b[i], conv_w_pw2[i], conv_b_pw2[i])
        else:
            i = layer - N_A
            q = split_heads(u @ attn_w_q[i])
            o = stick_breaking_attention(q, k_shared, v_shared)
            o = o.transpose(0, 2, 1, 3).reshape(b, s, d)
            h = h + o @ attn_w_o[i]
        h = h + HALF_STEP * swiglu_ffn(rms_norm(h, ffn2_norm[layer]),
                                       ffn2_w_gate[layer], ffn2_w_up[layer], ffn2_w_down[layer])
        if layer == N_A - 1:
            kv = rms_norm(h, kv_norm) @ w_kv
            k_flat, v_flat = jnp.split(kv, 2, axis=-1)
            k_shared = split_heads(k_flat)
            v_shared = split_heads(v_flat)
    return rms_norm(h, final_norm)
```

```python
import functools

import jax
import jax.numpy as jnp
from jax import lax
from jax.experimental import pallas as pl
from jax.experimental.pallas import tpu as pltpu

RMS_EPS = 1e-6
LN_EPS = 1e-5
HALF_STEP = 0.5
N_HEADS = 16
CONV_HALO = 32
HEADS_PER_STEP = 2
VMEM_LIMIT = 56 * 1024 * 1024

F32 = jnp.float32
BF16 = jnp.bfloat16


def _rms(xf, g):
    ms = jnp.mean(xf * xf, axis=-1, keepdims=True)
    return xf * lax.rsqrt(ms + RMS_EPS) * g


def _resident(shape):
    zeros = (0,) * len(shape)
    return pl.BlockSpec(shape, lambda *_: zeros, pipeline_mode=pl.Buffered(1))


def _params(*sem):
    return pltpu.CompilerParams(dimension_semantics=sem, vmem_limit_bytes=VMEM_LIMIT)


def _ffn_kernel(*refs, n_chunks, final_norm):
    if final_norm:
        x_ref, g_ref, wg_ref, wu_ref, wd_ref, fg_ref, o_ref, xn_ref, acc_ref = refs
    else:
        x_ref, g_ref, wg_ref, wu_ref, wd_ref, o_ref, xn_ref, acc_ref = refs
    xn_ref[...] = _rms(x_ref[...], g_ref[...]).astype(BF16)
    acc_ref[...] = jnp.zeros_like(acc_ref)

    def body(c, carry):
        xn = xn_ref[...]
        g = jnp.dot(xn, wg_ref[c], preferred_element_type=F32)
        u = jnp.dot(xn, wu_ref[c], preferred_element_type=F32)
        hm = (g * jax.nn.sigmoid(g) * u).astype(BF16)
        acc_ref[...] += jnp.dot(hm, wd_ref[c], preferred_element_type=F32)
        return carry

    lax.fori_loop(0, n_chunks, body, 0)
    y = x_ref[...] + HALF_STEP * acc_ref[...]
    if final_norm:
        y = _rms(y, fg_ref[...])
    o_ref[...] = y


def _ffn(h, norm_g, wg, wu, wd, final_g=None, *, tm=512):
    m, d = h.shape
    n_chunks, _, tf = wg.shape
    final_norm = final_g is not None
    row = pl.BlockSpec((tm, d), lambda i: (i, 0))
    vec = _resident((1, d))
    in_specs = [row, vec, _resident(wg.shape), _resident(wu.shape), _resident(wd.shape)]
    args = [h, norm_g, wg, wu, wd]
    if final_norm:
        in_specs.append(vec)
        args.append(final_g)
    return pl.pallas_call(
        functools.partial(_ffn_kernel, n_chunks=n_chunks, final_norm=final_norm),
        grid=(m // tm,),
        in_specs=in_specs,
        out_specs=row,
        out_shape=jax.ShapeDtypeStruct((m, d), F32),
        scratch_shapes=[pltpu.VMEM((tm, d), BF16), pltpu.VMEM((tm, d), F32)],
        compiler_params=_params("parallel"),
        name="ffn_final" if final_norm else "ffn",
    )(*args)


def _norm_matmul_kernel(x_ref, g_ref, w_ref, o_ref):
    xn = _rms(x_ref[...], g_ref[...]).astype(BF16)
    o_ref[...] = jnp.dot(xn, w_ref[...], preferred_element_type=F32).astype(o_ref.dtype)


def _norm_matmul(h, norm_g, w, *, tm=512):
    m, d = h.shape
    n = w.shape[1]
    return pl.pallas_call(
        _norm_matmul_kernel,
        grid=(m // tm,),
        in_specs=[pl.BlockSpec((tm, d), lambda i: (i, 0)), _resident((1, d)), _resident(w.shape)],
        out_specs=pl.BlockSpec((tm, n), lambda i: (i, 0)),
        out_shape=jax.ShapeDtypeStruct((m, n), BF16),
        compiler_params=_params("parallel"),
        name="norm_matmul",
    )(h, norm_g, w)


def _proj_residual_kernel(h_ref, o_ref, w_ref, out_ref):
    out_ref[...] = h_ref[...] + jnp.dot(o_ref[...], w_ref[...], preferred_element_type=F32)


def _proj_residual(h, o, w, *, tm=512):
    m, d = h.shape
    row = pl.BlockSpec((tm, d), lambda i: (i, 0))
    return pl.pallas_call(
        _proj_residual_kernel,
        grid=(m // tm,),
        in_specs=[row, pl.BlockSpec((tm, o.shape[1]), lambda i: (i, 0)), _resident(w.shape)],
        out_specs=row,
        out_shape=jax.ShapeDtypeStruct((m, d), F32),
        compiler_params=_params("parallel"),
        name="proj_residual",
    )(h, o, w)


def _conv_kernel(x_ref, ng_ref, w1_ref, b1_ref, wdw_ref, bdw_ref, lng_ref, lnb_ref, w2_ref, b2_ref,
                 o_ref, gbuf_ref, cbuf_ref, *, tiles_per_seq, width, col_chunk, row_chunk):
    tm, d = x_ref.shape
    i = pl.program_id(0)

    @pl.when(i % tiles_per_seq == 0)
    def _():
        gbuf_ref[0:CONV_HALO, :] = jnp.zeros((CONV_HALO, d), F32)

    @pl.when(i % tiles_per_seq != 0)
    def _():
        gbuf_ref[0:CONV_HALO, :] = gbuf_ref[tm:tm + CONV_HALO, :]

    x = x_ref[...]
    u = _rms(x, ng_ref[...]).astype(BF16)
    for c in range(d // col_chunk):
        lo, hi = c * col_chunk, (c + 1) * col_chunk
        a = jnp.dot(u, w1_ref[:, lo:hi], preferred_element_type=F32) + b1_ref[:, lo:hi]
        gate = jnp.dot(u, w1_ref[:, d + lo:d + hi], preferred_element_type=F32) + b1_ref[:, d + lo:d + hi]
        gbuf_ref[CONV_HALO:CONV_HALO + tm, lo:hi] = a * jax.nn.sigmoid(gate)

    base = CONV_HALO - (width - 1)
    for r0 in range(0, tm, row_chunk):
        for c in range(d // col_chunk):
            lo, hi = c * col_chunk, (c + 1) * col_chunk
            acc = jnp.broadcast_to(bdw_ref[:, lo:hi], (row_chunk, col_chunk))
            for w in range(width):
                acc = acc + gbuf_ref[r0 + base + w:r0 + base + w + row_chunk, lo:hi] * wdw_ref[w:w + 1, lo:hi]
            cbuf_ref[r0:r0 + row_chunk, lo:hi] = acc

    hc = cbuf_ref[...]
    mu = jnp.mean(hc, axis=-1, keepdims=True)
    xc = hc - mu
    var = jnp.mean(xc * xc, axis=-1, keepdims=True)
    y = xc * lax.rsqrt(var + LN_EPS) * lng_ref[...] + lnb_ref[...]
    y = (y * jax.nn.sigmoid(y)).astype(BF16)
    o_ref[...] = x + jnp.dot(y, w2_ref[...], preferred_element_type=F32) + b2_ref[...]


def _conv_module(h, norm_g, w1, b1, wdw, bdw, lng, lnb, w2, b2, *, seq, tm=512):
    m, d = h.shape
    width = wdw.shape[0]
    assert width - 1 <= CONV_HALO and seq % tm == 0 and tm >= 2 * CONV_HALO
    row = pl.BlockSpec((tm, d), lambda i: (i, 0))
    vec = _resident((1, d))
    return pl.pallas_call(
        functools.partial(_conv_kernel, tiles_per_seq=seq // tm, width=width, col_chunk=256, row_chunk=32),
        grid=(m // tm,),
        in_specs=[row, vec, _resident(w1.shape), _resident(b1.shape), _resident(wdw.shape), vec, vec, vec,
                  _resident(w2.shape), vec],
        out_specs=row,
        out_shape=jax.ShapeDtypeStruct((m, d), F32),
        scratch_shapes=[pltpu.VMEM((CONV_HALO + tm, d), F32), pltpu.VMEM((tm, d), F32)],
        compiler_params=_params("arbitrary"),
        name="conv_module",
    )(h, norm_g, w1, b1, wdw, bdw, lng, lnb, w2, b2)


def _attn_kernel(q_ref, k_ref, v_ref, tri_ref, o_ref, acc_ref, *, blk, head_dim):
    seq, lanes = q_ref.shape
    lane = lax.broadcasted_iota(jnp.int32, (blk, lanes), 1)
    first_head = lane < head_dim
    causal = (lax.broadcasted_iota(jnp.int32, (blk, blk), 1)
              < lax.broadcasted_iota(jnp.int32, (blk, blk), 0))

    def block(qh, k0, carry, diag):
        kb = k_ref[pl.ds(k0, blk), :]
        vb = v_ref[pl.ds(k0, blk), :]
        z = lax.dot_general(qh, kb, (((1,), (1,)), ((), ())), preferred_element_type=F32)
        sp = jnp.maximum(z, 0.0) + jnp.log(1.0 + jnp.exp(-jnp.abs(z)))
        if diag:
            sp = jnp.where(causal, sp, 0.0)
        hi = sp.astype(BF16)
        lo = (sp - hi.astype(F32)).astype(BF16)
        r = jnp.dot(jnp.concatenate([hi, lo], axis=1), tri_ref[...], preferred_element_type=F32)
        w = jnp.exp(z - r - carry)
        if diag:
            w = jnp.where(causal, w, 0.0)
        pv = jnp.dot(w.astype(BF16), vb, preferred_element_type=F32)
        return pv, carry + r[:, 0:1]

    def q_block(qi, _):
        q0 = pl.multiple_of(qi * blk, blk)
        q = q_ref[pl.ds(q0, blk), :]
        qa = jnp.where(first_head, q, jnp.zeros_like(q))
        qb = jnp.where(first_head, jnp.zeros_like(q), q)
        zero = jnp.zeros((blk, 1), F32)
        pva, ca = block(qa, q0, zero, True)
        pvb, cb = block(qb, q0, zero, True)
        acc_ref[0] = pva
        acc_ref[1] = pvb

        def k_step(j, carry):
            ca, cb = carry
            k0 = pl.multiple_of((qi - 1 - j) * blk, blk)
            pva, ca = block(qa, k0, ca, False)
            pvb, cb = block(qb, k0, cb, False)
            acc_ref[0] += pva
            acc_ref[1] += pvb
            return ca, cb

        lax.fori_loop(0, qi, k_step, (ca, cb))
        o_ref[pl.ds(q0, blk), :] = jnp.where(first_head, acc_ref[0], acc_ref[1]).astype(o_ref.dtype)
        return 0

    lax.fori_loop(0, seq // blk, q_block, 0)


def _attention(q, kv, tri, *, batch, seq, blk=256):
    m, d = q.shape
    lanes = 128
    head_dim = d // N_HEADS
    assert head_dim * HEADS_PER_STEP == lanes
    n_pairs = d // lanes
    q3 = q.reshape(batch, seq, d)
    kv3 = kv.reshape(batch, seq, 2 * d)
    blk_spec = lambda off: pl.BlockSpec((None, seq, lanes), lambda b, p: (b, 0, p + off))
    out = pl.pallas_call(
        functools.partial(_attn_kernel, blk=blk, head_dim=head_dim),
        grid=(batch, n_pairs),
        in_specs=[blk_spec(0), blk_spec(0), blk_spec(n_pairs), _resident(tri.shape)],
        out_specs=blk_spec(0),
        out_shape=jax.ShapeDtypeStruct((batch, seq, d), BF16),
        scratch_shapes=[pltpu.VMEM((HEADS_PER_STEP, blk, lanes), F32)],
        compiler_params=_params("parallel", "parallel"),
        name="stickbreaking_attention",
    )(q3, kv3, kv3, tri)
    return out.reshape(m, d)


def kernel(x, ffn1_norm, ffn1_w_gate, ffn1_w_up, ffn1_w_down, mix_norm, ffn2_norm, ffn2_w_gate, ffn2_w_up, ffn2_w_down, conv_w_pw1, conv_b_pw1, conv_w_dw, conv_b_dw, conv_ln_g, conv_ln_b, conv_w_pw2, conv_b_pw2, kv_norm, w_kv, attn_w_q, attn_w_o, final_norm):
    batch, seq, d = x.shape
    depth = ffn1_norm.shape[0]
    n_conv = conv_w_pw1.shape[0]
    d_ff = ffn1_w_gate.shape[-1]
    tf = 256
    n_chunks = d_ff // tf
    head_dim = d // N_HEADS
    attn_blk = 256

    def up_w(w):
        return w.astype(BF16).reshape(depth, d, n_chunks, tf).transpose(0, 2, 1, 3)

    def down_w(w):
        return w.astype(BF16).reshape(depth, n_chunks, tf, d)

    wg1, wu1, wd1 = up_w(ffn1_w_gate), up_w(ffn1_w_up), down_w(ffn1_w_down)
    wg2, wu2, wd2 = up_w(ffn2_w_gate), up_w(ffn2_w_up), down_w(ffn2_w_down)
    w_pw1, w_pw2 = conv_w_pw1.astype(BF16), conv_w_pw2.astype(BF16)
    w_kv_b = w_kv.astype(BF16)
    w_q = (attn_w_q * (head_dim ** -0.5)).astype(BF16)
    w_o = attn_w_o.astype(BF16)
    row = lambda v: v.reshape(1, -1)

    t = (lax.broadcasted_iota(jnp.int32, (attn_blk, attn_blk), 0)
         >= lax.broadcasted_iota(jnp.int32, (attn_blk, attn_blk), 1)).astype(BF16)
    tri = jnp.concatenate([t, t], axis=0)

    h = x.reshape(batch * seq, d)
    kv = None
    for layer in range(depth):
        h = _ffn(h, row(ffn1_norm[layer]), wg1[layer], wu1[layer], wd1[layer])
        if layer < n_conv:
            i = layer
            h = _conv_module(h, row(mix_norm[layer]), w_pw1[i], row(conv_b_pw1[i]), conv_w_dw[i],
                             row(conv_b_dw[i]), row(conv_ln_g[i]), row(conv_ln_b[i]), w_pw2[i],
                             row(conv_b_pw2[i]), seq=seq)
        else:
            i = layer - n_conv
            q = _norm_matmul(h, row(mix_norm[layer]), w_q[i])
            o = _attention(q, kv, tri, batch=batch, seq=seq, blk=attn_blk)
            h = _proj_residual(h, o, w_o[i])
        last = layer == depth - 1
        h = _ffn(h, row(ffn2_norm[layer]), wg2[layer], wu2[layer], wd2[layer],
                 row(final_norm) if last else None)
        if layer == n_conv - 1:
            kv = _norm_matmul(h, row(kv_norm), w_kv_b)
    return h.reshape(batch, seq, d)
```

```python
import functools
import math

import numpy as np
import jax
import jax.numpy as jnp
from jax import lax
from jax.experimental import pallas as pl
from jax.experimental.pallas import tpu as pltpu

RMS_EPS = 1e-6
LN_EPS = 1e-5
HALF_STEP = 0.5
N_HEADS = 16
LANES = 128
CONV_HALO = 32
HEADS_PER_STEP = 2
PIPE_SLOTS = 4
VMEM_LIMIT = 56 * 1024 * 1024

F32 = jnp.float32
BF16 = jnp.bfloat16


def _rms(xf, g):
    ms = jnp.mean(xf * xf, axis=-1, keepdims=True)
    return xf * lax.rsqrt(ms + RMS_EPS) * g


def _resident(shape):
    zeros = (0,) * len(shape)
    return pl.BlockSpec(shape, lambda *_: zeros, pipeline_mode=pl.Buffered(1))


def _params(*sem):
    return pltpu.CompilerParams(dimension_semantics=sem, vmem_limit_bytes=VMEM_LIMIT)


def _ffn_kernel(*refs, tf, final_norm):
    if final_norm:
        x_ref, g_ref, wg_ref, wu_ref, wd_ref, fg_ref, o_ref, hm_ref = refs
    else:
        x_ref, g_ref, wg_ref, wu_ref, wd_ref, o_ref, hm_ref = refs
    xn = _rms(x_ref[...], g_ref[...]).astype(BF16)
    for lo in range(0, hm_ref.shape[1], tf):
        g = jnp.dot(xn, wg_ref[:, lo:lo + tf], preferred_element_type=F32)
        u = jnp.dot(xn, wu_ref[:, lo:lo + tf], preferred_element_type=F32)
        hm_ref[:, lo:lo + tf] = (g * jax.nn.sigmoid(g) * u).astype(BF16)
    y = x_ref[...] + HALF_STEP * jnp.dot(hm_ref[...], wd_ref[...], preferred_element_type=F32)
    if final_norm:
        y = _rms(y, fg_ref[...])
    o_ref[...] = y


def _ffn(h, norm_g, wg, wu, wd, final_g=None, *, tm=512, tf=256):
    m, d = h.shape
    d_ff = wg.shape[1]
    final_norm = final_g is not None
    row = pl.BlockSpec((tm, d), lambda i: (i, 0))
    vec = _resident((1, d))
    in_specs = [row, vec, _resident(wg.shape), _resident(wu.shape), _resident(wd.shape)]
    args = [h, norm_g, wg, wu, wd]
    if final_norm:
        in_specs.append(vec)
        args.append(final_g)
    return pl.pallas_call(
        functools.partial(_ffn_kernel, tf=tf, final_norm=final_norm),
        grid=(m // tm,),
        in_specs=in_specs,
        out_specs=row,
        out_shape=jax.ShapeDtypeStruct((m, d), F32),
        scratch_shapes=[pltpu.VMEM((tm, d_ff), BF16)],
        compiler_params=_params("parallel"),
        name="ffn_final" if final_norm else "ffn",
    )(*args)


def _norm_matmul_kernel(x_ref, g_ref, w_ref, o_ref):
    xn = _rms(x_ref[...], g_ref[...]).astype(BF16)
    o_ref[...] = jnp.dot(xn, w_ref[...], preferred_element_type=F32).astype(o_ref.dtype)


def _norm_matmul(h, norm_g, w, *, tm=512):
    m, d = h.shape
    n = w.shape[1]
    return pl.pallas_call(
        _norm_matmul_kernel,
        grid=(m // tm,),
        in_specs=[pl.BlockSpec((tm, d), lambda i: (i, 0)), _resident((1, d)), _resident(w.shape)],
        out_specs=pl.BlockSpec((tm, n), lambda i: (i, 0)),
        out_shape=jax.ShapeDtypeStruct((m, n), BF16),
        compiler_params=_params("parallel"),
        name="norm_matmul",
    )(h, norm_g, w)


def _proj_residual_kernel(h_ref, o_ref, w_ref, out_ref):
    out_ref[...] = h_ref[...] + jnp.dot(o_ref[...], w_ref[...], preferred_element_type=F32)


def _proj_residual(h, o, w, *, tm=512):
    m, d = h.shape
    row = pl.BlockSpec((tm, d), lambda i: (i, 0))
    return pl.pallas_call(
        _proj_residual_kernel,
        grid=(m // tm,),
        in_specs=[row, pl.BlockSpec((tm, o.shape[1]), lambda i: (i, 0)), _resident(w.shape)],
        out_specs=row,
        out_shape=jax.ShapeDtypeStruct((m, d), F32),
        compiler_params=_params("parallel"),
        name="proj_residual",
    )(h, o, w)


def _conv_kernel(x_ref, ng_ref, w1_ref, b1_ref, wdw_ref, bdw_ref, lng_ref, lnb_ref, w2_ref, b2_ref,
                 o_ref, gbuf_ref, cbuf_ref, *, tiles_per_seq, width, col_chunk, row_chunk):
    tm, d = x_ref.shape
    i = pl.program_id(0)

    @pl.when(i % tiles_per_seq == 0)
    def _():
        gbuf_ref[0:CONV_HALO, :] = jnp.zeros((CONV_HALO, d), F32)

    @pl.when(i % tiles_per_seq != 0)
    def _():
        gbuf_ref[0:CONV_HALO, :] = gbuf_ref[tm:tm + CONV_HALO, :]

    x = x_ref[...]
    u = _rms(x, ng_ref[...]).astype(BF16)
    for c in range(d // col_chunk):
        lo, hi = c * col_chunk, (c + 1) * col_chunk
        a = jnp.dot(u, w1_ref[:, lo:hi], preferred_element_type=F32) + b1_ref[:, lo:hi]
        gate = jnp.dot(u, w1_ref[:, d + lo:d + hi], preferred_element_type=F32) + b1_ref[:, d + lo:d + hi]
        gbuf_ref[CONV_HALO:CONV_HALO + tm, lo:hi] = a * jax.nn.sigmoid(gate)

    base = CONV_HALO - (width - 1)
    for r0 in range(0, tm, row_chunk):
        for c in range(d // col_chunk):
            lo, hi = c * col_chunk, (c + 1) * col_chunk
            acc = jnp.broadcast_to(bdw_ref[:, lo:hi], (row_chunk, col_chunk))
            for w in range(width):
                acc = acc + gbuf_ref[r0 + base + w:r0 + base + w + row_chunk, lo:hi] * wdw_ref[w:w + 1, lo:hi]
            cbuf_ref[r0:r0 + row_chunk, lo:hi] = acc

    hc = cbuf_ref[...]
    mu = jnp.mean(hc, axis=-1, keepdims=True)
    xc = hc - mu
    var = jnp.mean(xc * xc, axis=-1, keepdims=True)
    y = xc * lax.rsqrt(var + LN_EPS) * lng_ref[...] + lnb_ref[...]
    y = (y * jax.nn.sigmoid(y)).astype(BF16)
    o_ref[...] = x + jnp.dot(y, w2_ref[...], preferred_element_type=F32) + b2_ref[...]


def _conv_module(h, norm_g, w1, b1, wdw, bdw, lng, lnb, w2, b2, *, seq, tm=512):
    m, d = h.shape
    width = wdw.shape[0]
    assert width - 1 <= CONV_HALO and seq % tm == 0 and tm >= 2 * CONV_HALO
    row = pl.BlockSpec((tm, d), lambda i: (i, 0))
    vec = _resident((1, d))
    return pl.pallas_call(
        functools.partial(_conv_kernel, tiles_per_seq=seq // tm, width=width, col_chunk=256, row_chunk=32),
        grid=(m // tm,),
        in_specs=[row, vec, _resident(w1.shape), _resident(b1.shape), _resident(wdw.shape), vec, vec, vec,
                  _resident(w2.shape), vec],
        out_specs=row,
        out_shape=jax.ShapeDtypeStruct((m, d), F32),
        scratch_shapes=[pltpu.VMEM((CONV_HALO + tm, d), F32), pltpu.VMEM((tm, d), F32)],
        compiler_params=_params("arbitrary"),
        name="conv_module",
    )(h, norm_g, w1, b1, wdw, bdw, lng, lnb, w2, b2)


def _attn_kernel(qi_tab, kj_tab, k_ref, qt_ref, vt_ref, u2_ref, bias_ref, ot_ref,
                 qm_ref, zbuf, hbuf, tbuf, wbuf, tot_ref, c_ref, acc_ref, *, head_dim, n_pairs):
    blk = u2_ref.shape[0]
    row = lax.broadcasted_iota(jnp.int32, qt_ref.shape, 1)
    qt = qt_ref[...]
    qm_ref[0] = jnp.where(row < head_dim, qt, jnp.zeros_like(qt))
    qm_ref[1] = jnp.where(row < head_dim, jnp.zeros_like(qt), qt)
    c_ref[...] = jnp.zeros_like(c_ref)
    acc_ref[...] = jnp.zeros_like(acc_ref)
    heads = range(HEADS_PER_STEP)

    def scores(p, slot):
        qi, kj = qi_tab[p], kj_tab[p]
        kb = k_ref[pl.ds(pl.multiple_of(kj * blk, blk), blk), :]
        bias = bias_ref[(qi == kj).astype(jnp.int32)]
        for h in heads:
            zbuf[slot, h] = jnp.dot(kb, qm_ref[h, qi], preferred_element_type=F32) + bias

    def softplus_split(p, slot):
        for h in heads:
            z = zbuf[slot, h]
            sp = jnp.maximum(z, 0.0) + jnp.log2(1.0 + jnp.exp2(-jnp.abs(z)))
            hi = sp.astype(BF16)
            hbuf[slot, h, 0:blk, :] = hi
            hbuf[slot, h, blk:2 * blk, :] = (sp - hi.astype(F32)).astype(BF16)

    def later_key_sums(p, slot):
        for h in heads:
            r = jnp.dot(u2_ref[...], hbuf[slot, h], preferred_element_type=F32)
            tbuf[slot, h] = zbuf[slot, h] - r
            tot_ref[slot, h] = r[0:1, :]

    def weights(p, slot):
        for h in heads:
            wbuf[slot, h] = jnp.exp2(tbuf[slot, h]).astype(BF16)

    def values(p, slot):
        qi, kj = qi_tab[p], kj_tab[p]
        first = qi == kj
        for h in heads:
            rows = slice(h * head_dim, (h + 1) * head_dim)
            pv = jnp.dot(vt_ref[kj, rows, :], wbuf[slot, h], preferred_element_type=F32)
            c = jnp.where(first, 0.0, c_ref[h])
            acc = jnp.where(first, 0.0, acc_ref[h]) + pv * jnp.exp2(-c)
            acc_ref[h] = acc
            c_ref[h] = c + tot_ref[slot, h]
            ot_ref[qi, rows, :] = acc.astype(ot_ref.dtype)

    stages = (scores, softplus_split, later_key_sums, weights, values)

    def step(p, p_mod, lo, hi):
        for k, stage in enumerate(stages):
            if lo <= k and k < hi:
                stage(p - k, (p_mod - k) % PIPE_SLOTS)

    depth = len(stages)
    main_start = PIPE_SLOTS * ((depth - 1 + PIPE_SLOTS - 1) // PIPE_SLOTS)
    main_end = n_pairs - n_pairs % PIPE_SLOTS
    assert main_start <= main_end
    for p in range(main_start):
        step(p, p % PIPE_SLOTS, 0, p + 1)

    def body(i, carry):
        base = main_start + i * PIPE_SLOTS
        for d in range(PIPE_SLOTS):
            step(base + d, d, 0, depth)
        return carry

    lax.fori_loop(0, (main_end - main_start) // PIPE_SLOTS, body, 0)
    for p in range(main_end, n_pairs + depth - 1):
        step(p, p % PIPE_SLOTS, max(0, p - n_pairs + 1), depth)


def _attention(q, kv, *, batch, seq, blk=256):
    m, d = q.shape
    head_dim = d // N_HEADS
    assert head_dim * HEADS_PER_STEP == LANES and seq % blk == 0
    n_hp = d // LANES
    n_blk = seq // blk
    pairs = [(qi, kj) for qi in range(n_blk) for kj in range(qi, -1, -1)]
    assert len(pairs) >= 3
    qi_tab = jnp.asarray(np.array([p[0] for p in pairs], np.int32))
    kj_tab = jnp.asarray(np.array([p[1] for p in pairs], np.int32))

    upper = (lax.broadcasted_iota(jnp.int32, (blk, blk), 1)
             >= lax.broadcasted_iota(jnp.int32, (blk, blk), 0))
    u2 = jnp.concatenate([upper, upper], axis=1).astype(BF16)
    strictly_causal = (lax.broadcasted_iota(jnp.int32, (blk, blk), 0)
                       < lax.broadcasted_iota(jnp.int32, (blk, blk), 1))
    bias = jnp.stack([jnp.zeros((blk, blk), F32), jnp.where(strictly_causal, 0.0, -jnp.inf).astype(F32)])

    def transposed_blocks(t):
        return t.reshape(batch, n_blk, blk, n_hp, LANES).transpose(0, 3, 1, 4, 2)

    qt = transposed_blocks(q)
    vt = transposed_blocks(kv[:, d:])
    kv3 = kv.reshape(batch, seq, 2 * d)
    tspec = pl.BlockSpec((None, None, n_blk, LANES, blk), lambda b, p, *_: (b, p, 0, 0, 0))
    ot = pl.pallas_call(
        functools.partial(_attn_kernel, head_dim=head_dim, n_pairs=len(pairs)),
        grid_spec=pltpu.PrefetchScalarGridSpec(
            num_scalar_prefetch=2,
            grid=(batch, n_hp),
            in_specs=[pl.BlockSpec((None, seq, LANES), lambda b, p, *_: (b, 0, p)), tspec, tspec,
                      _resident(u2.shape), _resident(bias.shape)],
            out_specs=tspec,
            scratch_shapes=[
                pltpu.VMEM((HEADS_PER_STEP, n_blk, LANES, blk), BF16),
                pltpu.VMEM((PIPE_SLOTS, HEADS_PER_STEP, blk, blk), F32),
                pltpu.VMEM((PIPE_SLOTS, HEADS_PER_STEP, 2 * blk, blk), BF16),
                pltpu.VMEM((PIPE_SLOTS, HEADS_PER_STEP, blk, blk), F32),
                pltpu.VMEM((PIPE_SLOTS, HEADS_PER_STEP, blk, blk), BF16),
                pltpu.VMEM((PIPE_SLOTS, HEADS_PER_STEP, 1, blk), F32),
                pltpu.VMEM((HEADS_PER_STEP, 1, blk), F32),
                pltpu.VMEM((HEADS_PER_STEP, head_dim, blk), F32),
            ]),
        out_shape=jax.ShapeDtypeStruct((batch, n_hp, n_blk, LANES, blk), BF16),
        compiler_params=_params("parallel", "parallel"),
        name="stickbreaking_attention",
    )(qi_tab, kj_tab, kv3, qt, vt, u2, bias)
    return ot.transpose(0, 2, 4, 1, 3).reshape(m, d)


def kernel(x, ffn1_norm, ffn1_w_gate, ffn1_w_up, ffn1_w_down, mix_norm, ffn2_norm, ffn2_w_gate, ffn2_w_up, ffn2_w_down, conv_w_pw1, conv_b_pw1, conv_w_dw, conv_b_dw, conv_ln_g, conv_ln_b, conv_w_pw2, conv_b_pw2, kv_norm, w_kv, attn_w_q, attn_w_o, final_norm):
    batch, seq, d = x.shape
    depth = ffn1_norm.shape[0]
    n_conv = conv_w_pw1.shape[0]
    head_dim = d // N_HEADS

    cast = lambda w: w.astype(BF16)
    wg1, wu1, wd1 = cast(ffn1_w_gate), cast(ffn1_w_up), cast(ffn1_w_down)
    wg2, wu2, wd2 = cast(ffn2_w_gate), cast(ffn2_w_up), cast(ffn2_w_down)
    w_pw1, w_pw2 = cast(conv_w_pw1), cast(conv_w_pw2)
    w_kv_b = cast(w_kv)
    w_q = cast(attn_w_q * (math.log2(math.e) * head_dim ** -0.5))
    w_o = cast(attn_w_o)
    row = lambda v: v.reshape(1, -1)

    h = x.reshape(batch * seq, d)
    kv = None
    for layer in range(depth):
        h = _ffn(h, row(ffn1_norm[layer]), wg1[layer], wu1[layer], wd1[layer])
        if layer < n_conv:
            i = layer
            h = _conv_module(h, row(mix_norm[layer]), w_pw1[i], row(conv_b_pw1[i]), conv_w_dw[i],
                             row(conv_b_dw[i]), row(conv_ln_g[i]), row(conv_ln_b[i]), w_pw2[i],
                             row(conv_b_pw2[i]), seq=seq)
        else:
            i = layer - n_conv
            q = _norm_matmul(h, row(mix_norm[layer]), w_q[i])
            o = _attention(q, kv, batch=batch, seq=seq)
            h = _proj_residual(h, o, w_o[i])
        last = layer == depth - 1
        h = _ffn(h, row(ffn2_norm[layer]), wg2[layer], wu2[layer], wd2[layer],
                 row(final_norm) if last else None)
        if layer == n_conv - 1:
            kv = _norm_matmul(h, row(kv_norm), w_kv_b)
    return h.reshape(batch, seq, d)
```

```python
import functools
import math

import numpy as np
import jax
import jax.numpy as jnp
from jax import lax
from jax.experimental import pallas as pl
from jax.experimental.pallas import tpu as pltpu

RMS_EPS = 1e-6
LN_EPS = 1e-5
HALF_STEP = 0.5
N_HEADS = 16
LANES = 128
SUBLANES = 8
CONV_HALO = 32
HEADS_PER_STEP = 2
PIPE_SLOTS = 4
VMEM_LIMIT = 56 * 1024 * 1024

F32 = jnp.float32
BF16 = jnp.bfloat16


def _rms(xf, g):
    ms = jnp.mean(xf * xf, axis=-1, keepdims=True)
    return xf * lax.rsqrt(ms + RMS_EPS) * g


def _resident(shape):
    zeros = (0,) * len(shape)
    return pl.BlockSpec(shape, lambda *_: zeros, pipeline_mode=pl.Buffered(1))


def _params(*sem):
    return pltpu.CompilerParams(dimension_semantics=sem, vmem_limit_bytes=VMEM_LIMIT)


def _ffn_kernel(*refs, tf, final_norm):
    if final_norm:
        x_ref, g_ref, wg_ref, wu_ref, wd_ref, fg_ref, o_ref, hm_ref = refs
    else:
        x_ref, g_ref, wg_ref, wu_ref, wd_ref, o_ref, hm_ref = refs
    xn = _rms(x_ref[...], g_ref[...]).astype(BF16)
    for lo in range(0, hm_ref.shape[1], tf):
        g = jnp.dot(xn, wg_ref[:, lo:lo + tf], preferred_element_type=F32)
        u = jnp.dot(xn, wu_ref[:, lo:lo + tf], preferred_element_type=F32)
        hm_ref[:, lo:lo + tf] = (g * jax.nn.sigmoid(g) * u).astype(BF16)
    y = x_ref[...] + HALF_STEP * jnp.dot(hm_ref[...], wd_ref[...], preferred_element_type=F32)
    if final_norm:
        y = _rms(y, fg_ref[...])
    o_ref[...] = y


def _ffn(h, norm_g, wg, wu, wd, final_g=None, *, tm=512, tf=256):
    m, d = h.shape
    d_ff = wg.shape[1]
    final_norm = final_g is not None
    row = pl.BlockSpec((tm, d), lambda i: (i, 0))
    vec = _resident((1, d))
    in_specs = [row, vec, _resident(wg.shape), _resident(wu.shape), _resident(wd.shape)]
    args = [h, norm_g, wg, wu, wd]
    if final_norm:
        in_specs.append(vec)
        args.append(final_g)
    return pl.pallas_call(
        functools.partial(_ffn_kernel, tf=tf, final_norm=final_norm),
        grid=(m // tm,),
        in_specs=in_specs,
        out_specs=row,
        out_shape=jax.ShapeDtypeStruct((m, d), F32),
        scratch_shapes=[pltpu.VMEM((tm, d_ff), BF16)],
        compiler_params=_params("parallel"),
        name="ffn_final" if final_norm else "ffn",
    )(*args)


NT_DIMS = (((1,), (1,)), ((), ()))


def _store_transposed(ref, res_t, blk):
    for hp in range(ref.shape[0]):
        for j in range(ref.shape[1]):
            ref[hp, j] = res_t[hp * LANES:(hp + 1) * LANES, j * blk:(j + 1) * blk].astype(ref.dtype)


def _q_proj_kernel(x_ref, g_ref, wt_ref, qt_ref, *, blk):
    xn = _rms(x_ref[...], g_ref[...]).astype(BF16)
    _store_transposed(qt_ref, lax.dot_general(wt_ref[...], xn, NT_DIMS, preferred_element_type=F32), blk)


def _kv_proj_kernel(x_ref, g_ref, wk_ref, wvt_ref, k_ref, vt_ref, *, blk):
    xn = _rms(x_ref[...], g_ref[...]).astype(BF16)
    k = jnp.dot(xn, wk_ref[...], preferred_element_type=F32)
    for hp in range(k_ref.shape[0]):
        k_ref[hp] = k[:, hp * LANES:(hp + 1) * LANES].astype(k_ref.dtype)
    _store_transposed(vt_ref, lax.dot_general(wvt_ref[...], xn, NT_DIMS, preferred_element_type=F32), blk)


def _attn_layout_specs(d, tm, blk):
    n_hp = d // LANES
    natural = pl.BlockSpec((None, n_hp, tm, LANES), lambda b, i: (b, 0, i, 0))
    transposed = pl.BlockSpec((None, n_hp, tm // blk, LANES, blk), lambda b, i: (b, 0, i, 0, 0))
    return n_hp, natural, transposed


def _q_proj(h, norm_g, wt, *, batch, seq, blk, tm=512):
    d = h.shape[1]
    n_hp, _, transposed = _attn_layout_specs(d, tm, blk)
    tiles = seq // tm
    return pl.pallas_call(
        functools.partial(_q_proj_kernel, blk=blk),
        grid=(batch, tiles),
        in_specs=[pl.BlockSpec((tm, d), lambda b, i: (b * tiles + i, 0)), _resident((1, d)), _resident(wt.shape)],
        out_specs=transposed,
        out_shape=jax.ShapeDtypeStruct((batch, n_hp, seq // blk, LANES, blk), BF16),
        compiler_params=_params("parallel", "parallel"),
        name="q_proj",
    )(h, norm_g, wt)


def _kv_proj(h, norm_g, wk, wvt, *, batch, seq, blk, tm=512):
    d = h.shape[1]
    n_hp, natural, transposed = _attn_layout_specs(d, tm, blk)
    tiles = seq // tm
    return pl.pallas_call(
        functools.partial(_kv_proj_kernel, blk=blk),
        grid=(batch, tiles),
        in_specs=[pl.BlockSpec((tm, d), lambda b, i: (b * tiles + i, 0)), _resident((1, d)),
                  _resident(wk.shape), _resident(wvt.shape)],
        out_specs=[natural, transposed],
        out_shape=[jax.ShapeDtypeStruct((batch, n_hp, seq, LANES), BF16),
                   jax.ShapeDtypeStruct((batch, n_hp, seq // blk, LANES, blk), BF16)],
        compiler_params=_params("parallel", "parallel"),
        name="kv_proj",
    )(h, norm_g, wk, wvt)


def _proj_residual_kernel(h_ref, ot_ref, w_ref, out_ref):
    n_hp, n_j, _, blk = ot_ref.shape
    for j in range(n_j):
        rows = slice(j * blk, (j + 1) * blk)
        o = jnp.concatenate([ot_ref[hp, j].astype(F32).T.astype(BF16) for hp in range(n_hp)], axis=1)
        out_ref[rows, :] = h_ref[rows, :] + jnp.dot(o, w_ref[...], preferred_element_type=F32)


def _proj_residual(h, ot, w, *, tm=512):
    m, d = h.shape
    batch, _, n_blk, _, blk = ot.shape
    tiles = n_blk * blk // tm
    _, _, transposed = _attn_layout_specs(d, tm, blk)
    row = pl.BlockSpec((tm, d), lambda b, i: (b * tiles + i, 0))
    return pl.pallas_call(
        _proj_residual_kernel,
        grid=(batch, tiles),
        in_specs=[row, transposed, _resident(w.shape)],
        out_specs=row,
        out_shape=jax.ShapeDtypeStruct((m, d), F32),
        compiler_params=_params("parallel", "parallel"),
        name="proj_residual",
    )(h, ot, w)


def _conv_kernel(x_ref, ng_ref, w1_ref, b1_ref, wdw_ref, bdw_ref, lng_ref, lnb_ref, w2_ref, b2_ref,
                 o_ref, gbuf_ref, cbuf_ref, *, tiles_per_seq, width, col_chunk, row_chunk):
    tm, d = x_ref.shape
    i = pl.program_id(0)

    @pl.when(i % tiles_per_seq == 0)
    def _():
        gbuf_ref[0:CONV_HALO, :] = jnp.zeros((CONV_HALO, d), F32)

    @pl.when(i % tiles_per_seq != 0)
    def _():
        gbuf_ref[0:CONV_HALO, :] = gbuf_ref[tm:tm + CONV_HALO, :]

    x = x_ref[...]
    u = _rms(x, ng_ref[...]).astype(BF16)
    for c in range(d // col_chunk):
        lo, hi = c * col_chunk, (c + 1) * col_chunk
        a = jnp.dot(u, w1_ref[:, lo:hi], preferred_element_type=F32) + b1_ref[:, lo:hi]
        gate = jnp.dot(u, w1_ref[:, d + lo:d + hi], preferred_element_type=F32) + b1_ref[:, d + lo:d + hi]
        gbuf_ref[CONV_HALO:CONV_HALO + tm, lo:hi] = a * jax.nn.sigmoid(gate)

    pad = CONV_HALO - (width - 1)

    def conv_rows(i, carry):
        r0 = pl.multiple_of(i * row_chunk, row_chunk)
        for lo in range(0, d, LANES):
            lanes = slice(lo, lo + LANES)
            acc = jnp.broadcast_to(bdw_ref[:, lanes], (row_chunk, LANES))
            for b in range(SUBLANES):
                rows = row_chunk if b == 0 else row_chunk + SUBLANES
                part = None
                for a in range((pad + width - 1) // SUBLANES + 1):
                    w = SUBLANES * a + b - pad
                    if 0 <= w < width:
                        term = gbuf_ref[pl.ds(r0 + SUBLANES * a, rows), lanes] * wdw_ref[w:w + 1, lanes]
                        part = term if part is None else part + term
                acc = acc + part[b:b + row_chunk]
            cbuf_ref[pl.ds(r0, row_chunk), lanes] = acc
        return carry

    lax.fori_loop(0, tm // row_chunk, conv_rows, 0)

    hc = cbuf_ref[...]
    mu = jnp.mean(hc, axis=-1, keepdims=True)
    xc = hc - mu
    var = jnp.mean(xc * xc, axis=-1, keepdims=True)
    y = xc * lax.rsqrt(var + LN_EPS) * lng_ref[...] + lnb_ref[...]
    y = (y * jax.nn.sigmoid(y)).astype(BF16)
    o_ref[...] = x + jnp.dot(y, w2_ref[...], preferred_element_type=F32) + b2_ref[...]


def _conv_module(h, norm_g, w1, b1, wdw, bdw, lng, lnb, w2, b2, *, seq, tm=512):
    m, d = h.shape
    width = wdw.shape[0]
    assert width - 1 <= CONV_HALO and seq % tm == 0 and tm >= 2 * CONV_HALO
    row = pl.BlockSpec((tm, d), lambda i: (i, 0))
    vec = _resident((1, d))
    return pl.pallas_call(
        functools.partial(_conv_kernel, tiles_per_seq=seq // tm, width=width, col_chunk=256, row_chunk=64),
        grid=(m // tm,),
        in_specs=[row, vec, _resident(w1.shape), _resident(b1.shape), _resident(wdw.shape), vec, vec, vec,
                  _resident(w2.shape), vec],
        out_specs=row,
        out_shape=jax.ShapeDtypeStruct((m, d), F32),
        scratch_shapes=[pltpu.VMEM((CONV_HALO + tm, d), F32), pltpu.VMEM((tm, d), F32)],
        compiler_params=_params("arbitrary"),
        name="conv_module",
    )(h, norm_g, w1, b1, wdw, bdw, lng, lnb, w2, b2)


def _attn_kernel(hp_tab, qi_tab, kj_tab, k_ref, qt_ref, vt_ref, u2_ref, bias_ref, ot_ref,
                 zbuf, hbuf, tbuf, wbuf, tot_ref, c_ref, acc_ref, *, head_dim, n_pairs):
    blk = u2_ref.shape[0]
    q_row = lax.broadcasted_iota(jnp.int32, (LANES, blk), 0)
    c_ref[...] = jnp.zeros_like(c_ref)
    acc_ref[...] = jnp.zeros_like(acc_ref)
    heads = range(HEADS_PER_STEP)

    def scores(p, slot):
        hp, qi, kj = hp_tab[p], qi_tab[p], kj_tab[p]
        kb = k_ref[hp, pl.ds(pl.multiple_of(kj * blk, blk), blk), :]
        qt = qt_ref[hp, qi]
        bias = bias_ref[(qi == kj).astype(jnp.int32)]
        for h in heads:
            qh = jnp.where(q_row // head_dim == h, qt, jnp.zeros_like(qt))
            zbuf[slot, h] = jnp.dot(kb, qh, preferred_element_type=F32) + bias

    def softplus_split(p, slot):
        for h in heads:
            z = zbuf[slot, h]
            sp = jnp.maximum(z, 0.0) + jnp.log2(1.0 + jnp.exp2(-jnp.abs(z)))
            hi = sp.astype(BF16)
            hbuf[slot, h, 0:blk, :] = hi
            hbuf[slot, h, blk:2 * blk, :] = (sp - hi.astype(F32)).astype(BF16)

    def later_key_sums(p, slot):
        for h in heads:
            r = jnp.dot(u2_ref[...], hbuf[slot, h], preferred_element_type=F32)
            tbuf[slot, h] = zbuf[slot, h] - r
            tot_ref[slot, h] = r[0:1, :]

    def weights(p, slot):
        for h in heads:
            wbuf[slot, h] = jnp.exp2(tbuf[slot, h]).astype(BF16)

    def values(p, slot):
        hp, qi, kj = hp_tab[p], qi_tab[p], kj_tab[p]
        first = qi == kj
        for h in heads:
            rows = slice(h * head_dim, (h + 1) * head_dim)
            pv = jnp.dot(vt_ref[hp, kj, rows, :], wbuf[slot, h], preferred_element_type=F32)
            c = jnp.where(first, 0.0, c_ref[h])
            acc = jnp.where(first, 0.0, acc_ref[h]) + pv * jnp.exp2(-c)
            acc_ref[h] = acc
            c_ref[h] = c + tot_ref[slot, h]
            ot_ref[hp, qi, rows, :] = acc.astype(ot_ref.dtype)

    stages = (scores, softplus_split, later_key_sums, weights, values)

    depth = len(stages)

    def step(p, p_mod, lo, hi):
        for k in reversed(range(lo, hi)):
            stages[k](p - k, (p_mod - k) % PIPE_SLOTS)

    main_start = PIPE_SLOTS * ((depth - 1 + PIPE_SLOTS - 1) // PIPE_SLOTS)
    main_end = n_pairs - n_pairs % PIPE_SLOTS
    assert main_start <= main_end
    for p in range(main_start):
        step(p, p % PIPE_SLOTS, 0, min(p + 1, depth))

    def body(i, carry):
        base = main_start + i * PIPE_SLOTS
        for d in range(PIPE_SLOTS):
            step(base + d, d, 0, depth)
        return carry

    lax.fori_loop(0, (main_end - main_start) // PIPE_SLOTS, body, 0)
    for p in range(main_end, n_pairs + depth - 1):
        step(p, p % PIPE_SLOTS, max(0, p - n_pairs + 1), depth)


def _attention(qt, k, vt):
    batch, n_hp, n_blk, _, blk = qt.shape
    seq = n_blk * blk
    head_dim = LANES // HEADS_PER_STEP
    pairs = [(hp, qi, kj) for hp in range(n_hp) for qi in range(n_blk) for kj in range(qi, -1, -1)]
    tabs = [jnp.asarray(np.array([p[c] for p in pairs], np.int32)) for c in range(3)]

    upper = (lax.broadcasted_iota(jnp.int32, (blk, blk), 1)
             >= lax.broadcasted_iota(jnp.int32, (blk, blk), 0))
    u2 = jnp.concatenate([upper, upper], axis=1).astype(BF16)
    strictly_causal = (lax.broadcasted_iota(jnp.int32, (blk, blk), 0)
                       < lax.broadcasted_iota(jnp.int32, (blk, blk), 1))
    bias = jnp.stack([jnp.zeros((blk, blk), F32), jnp.where(strictly_causal, 0.0, -jnp.inf).astype(F32)])

    tspec = pl.BlockSpec((None, n_hp, n_blk, LANES, blk), lambda b, *_: (b, 0, 0, 0, 0))
    return pl.pallas_call(
        functools.partial(_attn_kernel, head_dim=head_dim, n_pairs=len(pairs)),
        grid_spec=pltpu.PrefetchScalarGridSpec(
            num_scalar_prefetch=len(tabs),
            grid=(batch,),
            in_specs=[pl.BlockSpec((None, n_hp, seq, LANES), lambda b, *_: (b, 0, 0, 0)), tspec, tspec,
                      _resident(u2.shape), _resident(bias.shape)],
            out_specs=tspec,
            scratch_shapes=[
                pltpu.VMEM((PIPE_SLOTS, HEADS_PER_STEP, blk, blk), F32),
                pltpu.VMEM((PIPE_SLOTS, HEADS_PER_STEP, 2 * blk, blk), BF16),
                pltpu.VMEM((PIPE_SLOTS, HEADS_PER_STEP, blk, blk), F32),
                pltpu.VMEM((PIPE_SLOTS, HEADS_PER_STEP, blk, blk), BF16),
                pltpu.VMEM((PIPE_SLOTS, HEADS_PER_STEP, 1, blk), F32),
                pltpu.VMEM((HEADS_PER_STEP, 1, blk), F32),
                pltpu.VMEM((HEADS_PER_STEP, head_dim, blk), F32),
            ]),
        out_shape=jax.ShapeDtypeStruct(qt.shape, BF16),
        compiler_params=_params("parallel"),
        name="stickbreaking_attention",
    )(*tabs, k, qt, vt, u2, bias)


def kernel(x, ffn1_norm, ffn1_w_gate, ffn1_w_up, ffn1_w_down, mix_norm, ffn2_norm, ffn2_w_gate, ffn2_w_up, ffn2_w_down, conv_w_pw1, conv_b_pw1, conv_w_dw, conv_b_dw, conv_ln_g, conv_ln_b, conv_w_pw2, conv_b_pw2, kv_norm, w_kv, attn_w_q, attn_w_o, final_norm):
    batch, seq, d = x.shape
    depth = ffn1_norm.shape[0]
    n_conv = conv_w_pw1.shape[0]
    head_dim = d // N_HEADS
    assert head_dim * HEADS_PER_STEP == LANES

    cast = lambda w: w.astype(BF16)
    q_scale = math.log2(math.e) * head_dim ** -0.5
    row = lambda v: v.reshape(1, -1)

    attn_blk = 256
    h = x.reshape(batch * seq, d)
    k = vt = None
    for layer in range(depth):
        h = _ffn(h, row(ffn1_norm[layer]), cast(ffn1_w_gate[layer]), cast(ffn1_w_up[layer]),
                 cast(ffn1_w_down[layer]))
        if layer < n_conv:
            i = layer
            h = _conv_module(h, row(mix_norm[layer]), cast(conv_w_pw1[i]), row(conv_b_pw1[i]), conv_w_dw[i],
                             row(conv_b_dw[i]), row(conv_ln_g[i]), row(conv_ln_b[i]), cast(conv_w_pw2[i]),
                             row(conv_b_pw2[i]), seq=seq)
        else:
            i = layer - n_conv
            qt = _q_proj(h, row(mix_norm[layer]), cast(attn_w_q[i].T * q_scale),
                         batch=batch, seq=seq, blk=attn_blk)
            h = _proj_residual(h, _attention(qt, k, vt), cast(attn_w_o[i]))
        last = layer == depth - 1
        h = _ffn(h, row(ffn2_norm[layer]), cast(ffn2_w_gate[layer]), cast(ffn2_w_up[layer]),
                 cast(ffn2_w_down[layer]), row(final_norm) if last else None)
        if layer == n_conv - 1:
            k, vt = _kv_proj(h, row(kv_norm), cast(w_kv[:, :d]), cast(w_kv[:, d:].T),
                             batch=batch, seq=seq, blk=attn_blk)
    return h.reshape(batch, seq, d)
```

```python
import functools
import math

import numpy as np
import jax
import jax.numpy as jnp
from jax import lax
from jax.experimental import pallas as pl
from jax.experimental.pallas import tpu as pltpu

RMS_EPS = 1e-6
LN_EPS = 1e-5
HALF_STEP = 0.5
N_HEADS = 16
LANES = 128
SUBLANES = 8
CONV_HALO = 32
HEADS_PER_STEP = 2
PIPE_SLOTS = 8
VMEM_LIMIT = 56 * 1024 * 1024

F32 = jnp.float32
BF16 = jnp.bfloat16


def _rms(xf, g):
    ms = jnp.mean(xf * xf, axis=-1, keepdims=True)
    return xf * lax.rsqrt(ms + RMS_EPS) * g


def _resident(shape):
    zeros = (0,) * len(shape)
    return pl.BlockSpec(shape, lambda *_: zeros, pipeline_mode=pl.Buffered(1))


def _params(*sem):
    return pltpu.CompilerParams(dimension_semantics=sem, vmem_limit_bytes=VMEM_LIMIT)


def _ffn_kernel(*refs, tf, final_norm):
    if final_norm:
        x_ref, g_ref, wg_ref, wu_ref, wd_ref, fg_ref, o_ref, hm_ref = refs
    else:
        x_ref, g_ref, wg_ref, wu_ref, wd_ref, o_ref, hm_ref = refs
    xn = _rms(x_ref[...], g_ref[...]).astype(BF16)
    for lo in range(0, hm_ref.shape[1], tf):
        g = jnp.dot(xn, wg_ref[:, lo:lo + tf], preferred_element_type=F32)
        u = jnp.dot(xn, wu_ref[:, lo:lo + tf], preferred_element_type=F32)
        hm_ref[:, lo:lo + tf] = (g * jax.nn.sigmoid(g) * u).astype(BF16)
    y = x_ref[...] + HALF_STEP * jnp.dot(hm_ref[...], wd_ref[...], preferred_element_type=F32)
    if final_norm:
        y = _rms(y, fg_ref[...])
    o_ref[...] = y


def _layer_resident(w, layer):
    zeros = (0,) * (w.ndim - 1)
    return pl.BlockSpec((None,) + w.shape[1:], lambda *_: (layer,) + zeros, pipeline_mode=pl.Buffered(1))


def _ffn(h, norm_g, wg, wu, wd, layer, final_g=None, *, tm=512, tf=256):
    m, d = h.shape
    d_ff = wg.shape[2]
    final_norm = final_g is not None
    row = pl.BlockSpec((tm, d), lambda i: (i, 0))
    vec = _resident((1, d))
    in_specs = [row, vec, _layer_resident(wg, layer), _layer_resident(wu, layer), _layer_resident(wd, layer)]
    args = [h, norm_g, wg, wu, wd]
    if final_norm:
        in_specs.append(vec)
        args.append(final_g)
    return pl.pallas_call(
        functools.partial(_ffn_kernel, tf=tf, final_norm=final_norm),
        grid=(m // tm,),
        in_specs=in_specs,
        out_specs=row,
        out_shape=jax.ShapeDtypeStruct((m, d), F32),
        scratch_shapes=[pltpu.VMEM((tm, d_ff), BF16)],
        compiler_params=_params("parallel"),
        name="ffn_final" if final_norm else "ffn",
    )(*args)


NT_DIMS = (((1,), (1,)), ((), ()))


def _store_transposed(ref, res_t, blk):
    for hp in range(ref.shape[0]):
        for j in range(ref.shape[1]):
            ref[hp, j] = res_t[hp * LANES:(hp + 1) * LANES, j * blk:(j + 1) * blk].astype(ref.dtype)


def _q_proj_kernel(x_ref, g_ref, wt_ref, qt_ref, *, blk):
    xn = _rms(x_ref[...], g_ref[...]).astype(BF16)
    _store_transposed(qt_ref, lax.dot_general(wt_ref[...], xn, NT_DIMS, preferred_element_type=F32), blk)


def _kv_proj_kernel(x_ref, g_ref, wk_ref, wvt_ref, k_ref, vt_ref, *, blk):
    xn = _rms(x_ref[...], g_ref[...]).astype(BF16)
    k = jnp.dot(xn, wk_ref[...], preferred_element_type=F32)
    for hp in range(k_ref.shape[0]):
        k_ref[hp] = k[:, hp * LANES:(hp + 1) * LANES].astype(k_ref.dtype)
    _store_transposed(vt_ref, lax.dot_general(wvt_ref[...], xn, NT_DIMS, preferred_element_type=F32), blk)


def _attn_layout_specs(d, tm, blk):
    n_hp = d // LANES
    natural = pl.BlockSpec((None, n_hp, tm, LANES), lambda b, i: (b, 0, i, 0))
    transposed = pl.BlockSpec((None, n_hp, tm // blk, LANES, blk), lambda b, i: (b, 0, i, 0, 0))
    return n_hp, natural, transposed


def _q_proj(h, norm_g, wt, *, batch, seq, blk, tm=512):
    d = h.shape[1]
    n_hp, _, transposed = _attn_layout_specs(d, tm, blk)
    tiles = seq // tm
    return pl.pallas_call(
        functools.partial(_q_proj_kernel, blk=blk),
        grid=(batch, tiles),
        in_specs=[pl.BlockSpec((tm, d), lambda b, i: (b * tiles + i, 0)), _resident((1, d)), _resident(wt.shape)],
        out_specs=transposed,
        out_shape=jax.ShapeDtypeStruct((batch, n_hp, seq // blk, LANES, blk), BF16),
        compiler_params=_params("parallel", "parallel"),
        name="q_proj",
    )(h, norm_g, wt)


def _kv_proj(h, norm_g, wk, wvt, *, batch, seq, blk, tm=512):
    d = h.shape[1]
    n_hp, natural, transposed = _attn_layout_specs(d, tm, blk)
    tiles = seq // tm
    return pl.pallas_call(
        functools.partial(_kv_proj_kernel, blk=blk),
        grid=(batch, tiles),
        in_specs=[pl.BlockSpec((tm, d), lambda b, i: (b * tiles + i, 0)), _resident((1, d)),
                  _resident(wk.shape), _resident(wvt.shape)],
        out_specs=[natural, transposed],
        out_shape=[jax.ShapeDtypeStruct((batch, n_hp, seq, LANES), BF16),
                   jax.ShapeDtypeStruct((batch, n_hp, seq // blk, LANES, blk), BF16)],
        compiler_params=_params("parallel", "parallel"),
        name="kv_proj",
    )(h, norm_g, wk, wvt)


def _proj_residual_kernel(h_ref, ot_ref, w_ref, out_ref):
    n_hp, n_j, _, blk = ot_ref.shape
    for j in range(n_j):
        rows = slice(j * blk, (j + 1) * blk)
        o = jnp.concatenate([ot_ref[hp, j].astype(F32).T.astype(BF16) for hp in range(n_hp)], axis=1)
        out_ref[rows, :] = h_ref[rows, :] + jnp.dot(o, w_ref[...], preferred_element_type=F32)


def _proj_residual(h, ot, w, *, tm=512):
    m, d = h.shape
    batch, _, n_blk, _, blk = ot.shape
    tiles = n_blk * blk // tm
    _, _, transposed = _attn_layout_specs(d, tm, blk)
    row = pl.BlockSpec((tm, d), lambda b, i: (b * tiles + i, 0))
    return pl.pallas_call(
        _proj_residual_kernel,
        grid=(batch, tiles),
        in_specs=[row, transposed, _resident(w.shape)],
        out_specs=row,
        out_shape=jax.ShapeDtypeStruct((m, d), F32),
        compiler_params=_params("parallel", "parallel"),
        name="proj_residual",
    )(h, ot, w)


def _conv_kernel(x_ref, ng_ref, w1_ref, b1_ref, wdw_ref, bdw_ref, lng_ref, lnb_ref, w2_ref, b2_ref,
                 o_ref, gbuf_ref, cbuf_ref, *, tiles_per_seq, width, col_chunk, row_chunk):
    tm, d = x_ref.shape
    i = pl.program_id(0)

    @pl.when(i % tiles_per_seq == 0)
    def _():
        gbuf_ref[0:CONV_HALO, :] = jnp.zeros((CONV_HALO, d), F32)

    @pl.when(i % tiles_per_seq != 0)
    def _():
        gbuf_ref[0:CONV_HALO, :] = gbuf_ref[tm:tm + CONV_HALO, :]

    x = x_ref[...]
    u = _rms(x, ng_ref[...]).astype(BF16)
    for c in range(d // col_chunk):
        lo, hi = c * col_chunk, (c + 1) * col_chunk
        a = jnp.dot(u, w1_ref[:, lo:hi], preferred_element_type=F32) + b1_ref[:, lo:hi]
        gate = jnp.dot(u, w1_ref[:, d + lo:d + hi], preferred_element_type=F32) + b1_ref[:, d + lo:d + hi]
        gbuf_ref[CONV_HALO:CONV_HALO + tm, lo:hi] = a * jax.nn.sigmoid(gate)

    pad = CONV_HALO - (width - 1)

    def conv_rows(i, carry):
        r0 = pl.multiple_of(i * row_chunk, row_chunk)
        for lo in range(0, d, LANES):
            lanes = slice(lo, lo + LANES)
            acc = jnp.broadcast_to(bdw_ref[:, lanes], (row_chunk, LANES))
            for b in range(SUBLANES):
                rows = row_chunk if b == 0 else row_chunk + SUBLANES
                part = None
                for a in range((pad + width - 1) // SUBLANES + 1):
                    w = SUBLANES * a + b - pad
                    if 0 <= w < width:
                        term = gbuf_ref[pl.ds(r0 + SUBLANES * a, rows), lanes] * wdw_ref[w:w + 1, lanes]
                        part = term if part is None else part + term
                acc = acc + part[b:b + row_chunk]
            cbuf_ref[pl.ds(r0, row_chunk), lanes] = acc
        return carry

    lax.fori_loop(0, tm // row_chunk, conv_rows, 0)

    hc = cbuf_ref[...]
    mu = jnp.mean(hc, axis=-1, keepdims=True)
    xc = hc - mu
    var = jnp.mean(xc * xc, axis=-1, keepdims=True)
    y = xc * lax.rsqrt(var + LN_EPS) * lng_ref[...] + lnb_ref[...]
    y = (y * jax.nn.sigmoid(y)).astype(BF16)
    o_ref[...] = x + jnp.dot(y, w2_ref[...], preferred_element_type=F32) + b2_ref[...]


def _conv_module(h, norm_g, w1, b1, wdw, bdw, lng, lnb, w2, b2, *, seq, tm=512):
    m, d = h.shape
    width = wdw.shape[0]
    assert width - 1 <= CONV_HALO and seq % tm == 0 and tm >= 2 * CONV_HALO
    row = pl.BlockSpec((tm, d), lambda i: (i, 0))
    vec = _resident((1, d))
    return pl.pallas_call(
        functools.partial(_conv_kernel, tiles_per_seq=seq // tm, width=width, col_chunk=256, row_chunk=64),
        grid=(m // tm,),
        in_specs=[row, vec, _resident(w1.shape), _resident(b1.shape), _resident(wdw.shape), vec, vec, vec,
                  _resident(w2.shape), vec],
        out_specs=row,
        out_shape=jax.ShapeDtypeStruct((m, d), F32),
        scratch_shapes=[pltpu.VMEM((CONV_HALO + tm, d), F32), pltpu.VMEM((tm, d), F32)],
        compiler_params=_params("arbitrary"),
        name="conv_module",
    )(h, norm_g, w1, b1, wdw, bdw, lng, lnb, w2, b2)


def _attn_kernel(hp_tab, qi_tab, kj_tab, k_ref, qt_ref, vt_ref, u_ref, bias_ref, ot_ref,
                 zbuf, hbuf, tbuf, wbuf, tot_ref, c_ref, acc_ref, *, head_dim, n_pairs):
    blk = u_ref.shape[0]
    q_row = lax.broadcasted_iota(jnp.int32, (LANES, blk), 0)
    c_ref[...] = jnp.zeros_like(c_ref)
    acc_ref[...] = jnp.zeros_like(acc_ref)
    heads = range(HEADS_PER_STEP)

    def scores(p, slot):
        hp, qi, kj = hp_tab[p], qi_tab[p], kj_tab[p]
        kb = k_ref[hp, pl.ds(pl.multiple_of(kj * blk, blk), blk), :]
        qt = qt_ref[hp, qi]
        bias = bias_ref[(qi == kj).astype(jnp.int32)]
        for h in heads:
            qh = jnp.where(q_row // head_dim == h, qt, jnp.zeros_like(qt))
            zbuf[slot, h] = jnp.dot(kb, qh, preferred_element_type=F32) + bias

    def softplus(p, slot):
        for h in heads:
            z = zbuf[slot, h]
            sp = jnp.maximum(z, 0.0) + jnp.log2(1.0 + jnp.exp2(-jnp.abs(z)))
            hbuf[slot, h] = sp.astype(BF16)
            zbuf[slot, h] = z - sp

    def later_key_sums(p, slot):
        for h in heads:
            r = jnp.dot(u_ref[...], hbuf[slot, h], preferred_element_type=F32)
            tbuf[slot, h] = (zbuf[slot, h] - r).astype(BF16)
            tot_ref[slot, h] = r[0:1, :] + hbuf[slot, h, 0:1, :].astype(F32)

    def weights(p, slot):
        for h in heads:
            wbuf[slot, h] = jnp.exp2(tbuf[slot, h])

    def values(p, slot):
        hp, qi, kj = hp_tab[p], qi_tab[p], kj_tab[p]
        first = qi == kj
        for h in heads:
            rows = slice(h * head_dim, (h + 1) * head_dim)
            pv = jnp.dot(vt_ref[hp, kj, rows, :], wbuf[slot, h], preferred_element_type=F32)
            c = jnp.where(first, 0.0, c_ref[h])
            acc = jnp.where(first, 0.0, acc_ref[h]) + pv * jnp.exp2(-c)
            acc_ref[h] = acc
            c_ref[h] = c + tot_ref[slot, h]
            ot_ref[hp, qi, rows, :] = acc.astype(ot_ref.dtype)

    stages = (scores, softplus, later_key_sums, weights, values)

    depth = len(stages)

    def step(p, p_mod, lo, hi):
        for k in reversed(range(lo, hi)):
            stages[k](p - k, (p_mod - k) % PIPE_SLOTS)

    main_start = PIPE_SLOTS * ((depth - 1 + PIPE_SLOTS - 1) // PIPE_SLOTS)
    main_end = n_pairs - n_pairs % PIPE_SLOTS
    assert main_start <= main_end
    for p in range(main_start):
        step(p, p % PIPE_SLOTS, 0, min(p + 1, depth))

    def body(i, carry):
        base = main_start + i * PIPE_SLOTS
        for d in range(PIPE_SLOTS):
            step(base + d, d, 0, depth)
        return carry

    lax.fori_loop(0, (main_end - main_start) // PIPE_SLOTS, body, 0)
    for p in range(main_end, n_pairs + depth - 1):
        step(p, p % PIPE_SLOTS, max(0, p - n_pairs + 1), depth)


def _attention(qt, k, vt):
    batch, n_hp, n_blk, _, blk = qt.shape
    seq = n_blk * blk
    head_dim = LANES // HEADS_PER_STEP
    pairs = [(hp, qi, kj) for hp in range(n_hp) for qi in range(n_blk) for kj in range(qi, -1, -1)]
    tabs = [jnp.asarray(np.array([p[c] for p in pairs], np.int32)) for c in range(3)]

    strictly_causal = (lax.broadcasted_iota(jnp.int32, (blk, blk), 0)
                       < lax.broadcasted_iota(jnp.int32, (blk, blk), 1))
    u = strictly_causal.astype(BF16)
    bias = jnp.stack([jnp.zeros((blk, blk), F32), jnp.where(strictly_causal, 0.0, -jnp.inf).astype(F32)])

    tspec = pl.BlockSpec((None, n_hp, n_blk, LANES, blk), lambda b, *_: (b, 0, 0, 0, 0))
    return pl.pallas_call(
        functools.partial(_attn_kernel, head_dim=head_dim, n_pairs=len(pairs)),
        grid_spec=pltpu.PrefetchScalarGridSpec(
            num_scalar_prefetch=len(tabs),
            grid=(batch,),
            in_specs=[pl.BlockSpec((None, n_hp, seq, LANES), lambda b, *_: (b, 0, 0, 0)), tspec, tspec,
                      _resident(u.shape), _resident(bias.shape)],
            out_specs=tspec,
            scratch_shapes=[
                pltpu.VMEM((PIPE_SLOTS, HEADS_PER_STEP, blk, blk), F32),
                pltpu.VMEM((PIPE_SLOTS, HEADS_PER_STEP, blk, blk), BF16),
                pltpu.VMEM((PIPE_SLOTS, HEADS_PER_STEP, blk, blk), BF16),
                pltpu.VMEM((PIPE_SLOTS, HEADS_PER_STEP, blk, blk), BF16),
                pltpu.VMEM((PIPE_SLOTS, HEADS_PER_STEP, 1, blk), F32),
                pltpu.VMEM((HEADS_PER_STEP, 1, blk), F32),
                pltpu.VMEM((HEADS_PER_STEP, head_dim, blk), F32),
            ]),
        out_shape=jax.ShapeDtypeStruct(qt.shape, BF16),
        compiler_params=_params("parallel"),
        name="stickbreaking_attention",
    )(*tabs, k, qt, vt, u, bias)


def kernel(x, ffn1_norm, ffn1_w_gate, ffn1_w_up, ffn1_w_down, mix_norm, ffn2_norm, ffn2_w_gate, ffn2_w_up, ffn2_w_down, conv_w_pw1, conv_b_pw1, conv_w_dw, conv_b_dw, conv_ln_g, conv_ln_b, conv_w_pw2, conv_b_pw2, kv_norm, w_kv, attn_w_q, attn_w_o, final_norm):
    batch, seq, d = x.shape
    depth = ffn1_norm.shape[0]
    n_conv = conv_w_pw1.shape[0]
    head_dim = d // N_HEADS
    assert head_dim * HEADS_PER_STEP == LANES

    cast = lambda w: w.astype(BF16)
    ffn1 = (cast(ffn1_w_gate), cast(ffn1_w_up), cast(ffn1_w_down))
    ffn2 = (cast(ffn2_w_gate), cast(ffn2_w_up), cast(ffn2_w_down))
    q_scale = math.log2(math.e) * head_dim ** -0.5
    row = lambda v: v.reshape(1, -1)

    attn_blk = 256
    h = x.reshape(batch * seq, d)
    k = vt = None
    for layer in range(depth):
        h = _ffn(h, row(ffn1_norm[layer]), *ffn1, layer)
        if layer < n_conv:
            i = layer
            h = _conv_module(h, row(mix_norm[layer]), cast(conv_w_pw1[i]), row(conv_b_pw1[i]), conv_w_dw[i],
                             row(conv_b_dw[i]), row(conv_ln_g[i]), row(conv_ln_b[i]), cast(conv_w_pw2[i]),
                             row(conv_b_pw2[i]), seq=seq)
        else:
            i = layer - n_conv
            qt = _q_proj(h, row(mix_norm[layer]), cast(attn_w_q[i].T * q_scale),
                         batch=batch, seq=seq, blk=attn_blk)
            h = _proj_residual(h, _attention(qt, k, vt), cast(attn_w_o[i]))
        last = layer == depth - 1
        h = _ffn(h, row(ffn2_norm[layer]), *ffn2, layer, row(final_norm) if last else None)
        if layer == n_conv - 1:
            k, vt = _kv_proj(h, row(kv_norm), cast(w_kv[:, :d]), cast(w_kv[:, d:].T),
                             batch=batch, seq=seq, blk=attn_blk)
    return h.reshape(batch, seq, d)
```

```python
import functools
import math

import numpy as np
import jax
import jax.numpy as jnp
from jax import lax
from jax.experimental import pallas as pl
from jax.experimental.pallas import tpu as pltpu

RMS_EPS = 1e-6
LN_EPS = 1e-5
HALF_STEP = 0.5
N_HEADS = 16
LANES = 128
SUBLANES = 8
CONV_HALO = 32
HEADS_PER_STEP = 2
PIPE_SLOTS = 8
VMEM_LIMIT = 56 * 1024 * 1024

F32 = jnp.float32
BF16 = jnp.bfloat16


def _rms(xf, g):
    ms = jnp.mean(xf * xf, axis=-1, keepdims=True)
    return xf * lax.rsqrt(ms + RMS_EPS) * g


def _resident(shape):
    zeros = (0,) * len(shape)
    return pl.BlockSpec(shape, lambda *_: zeros, pipeline_mode=pl.Buffered(1))


def _params(*sem):
    return pltpu.CompilerParams(dimension_semantics=sem, vmem_limit_bytes=VMEM_LIMIT)


NATURAL, TRANSPOSED = "natural", "transposed"
NT_DIMS = (((1,), (1,)), ((), ()))


def _ffn_kernel(*refs, tf, blk, pre_proj, post, final_norm):
    refs = list(refs)
    x_ref = refs.pop(0)
    if pre_proj:
        ot_ref, wo_ref = refs.pop(0), refs.pop(0)
    g_ref, wg_ref, wu_ref, wd_ref = (refs.pop(0) for _ in range(4))
    if final_norm:
        fg_ref = refs.pop(0)
    if post:
        pg_ref = refs.pop(0)
    post_w = [refs.pop(0) for _ in post]
    o_ref = refs.pop(0)
    post_out = [refs.pop(0) for _ in post]
    hm_ref = refs.pop(0)
    if pre_proj:
        xs_ref = refs.pop(0)
        n_hp = ot_ref.shape[0]
        for j in range(ot_ref.shape[1]):
            rows = slice(j * blk, (j + 1) * blk)
            o = jnp.concatenate([ot_ref[hp, j].astype(F32).T.astype(BF16) for hp in range(n_hp)], axis=1)
            xs_ref[rows, :] = x_ref[rows, :] + jnp.dot(o, wo_ref[...], preferred_element_type=F32)
        x_ref = xs_ref

    xn = _rms(x_ref[...], g_ref[...]).astype(BF16)
    for lo in range(0, hm_ref.shape[1], tf):
        g = jnp.dot(xn, wg_ref[:, lo:lo + tf], preferred_element_type=F32)
        u = jnp.dot(xn, wu_ref[:, lo:lo + tf], preferred_element_type=F32)
        hm_ref[:, lo:lo + tf] = (g * jax.nn.sigmoid(g) * u).astype(BF16)
    y = x_ref[...] + HALF_STEP * jnp.dot(hm_ref[...], wd_ref[...], preferred_element_type=F32)
    if final_norm:
        y = _rms(y, fg_ref[...])
    o_ref[...] = y

    if post:
        yn = _rms(y, pg_ref[...]).astype(BF16)
    for layout, w_ref, out_ref in zip(post, post_w, post_out):
        if layout == NATURAL:
            res = jnp.dot(yn, w_ref[...], preferred_element_type=F32)
            for hp in range(out_ref.shape[0]):
                out_ref[hp] = res[:, hp * LANES:(hp + 1) * LANES].astype(out_ref.dtype)
        else:
            res_t = lax.dot_general(w_ref[...], yn, NT_DIMS, preferred_element_type=F32)
            for hp in range(out_ref.shape[0]):
                for j in range(out_ref.shape[1]):
                    out_ref[hp, j] = res_t[hp * LANES:(hp + 1) * LANES, j * blk:(j + 1) * blk].astype(out_ref.dtype)


def _layer_resident(w, layer):
    zeros = (0,) * (w.ndim - 1)
    return pl.BlockSpec((None,) + w.shape[1:], lambda *_: (layer,) + zeros, pipeline_mode=pl.Buffered(1))


def _ffn(h, norm_g, wg, wu, wd, layer, *, batch, seq, blk, pre=None, final_g=None, post_g=None, post=(),
         tm=512, tf=256):
    m, d = h.shape
    d_ff = wg.shape[2]
    n_hp = d // LANES
    tiles = seq // tm
    row = pl.BlockSpec((tm, d), lambda b, i: (b * tiles + i, 0))
    vec = _resident((1, d))
    layout_spec = {
        NATURAL: pl.BlockSpec((None, n_hp, tm, LANES), lambda b, i: (b, 0, i, 0)),
        TRANSPOSED: pl.BlockSpec((None, n_hp, tm // blk, LANES, blk), lambda b, i: (b, 0, i, 0, 0)),
    }
    layout_shape = {NATURAL: (batch, n_hp, seq, LANES), TRANSPOSED: (batch, n_hp, seq // blk, LANES, blk)}

    in_specs, args = [row], [h]
    if pre is not None:
        in_specs += [layout_spec[TRANSPOSED], _resident(pre[1].shape)]
        args += list(pre)
    in_specs += [vec, _layer_resident(wg, layer), _layer_resident(wu, layer), _layer_resident(wd, layer)]
    args += [norm_g, wg, wu, wd]
    if final_g is not None:
        in_specs.append(vec)
        args.append(final_g)
    if post:
        in_specs.append(vec)
        args.append(post_g)
    in_specs += [_resident(w.shape) for _, w in post]
    args += [w for _, w in post]
    layouts = tuple(layout for layout, _ in post)
    scratch = [pltpu.VMEM((tm, d_ff), BF16)]
    if pre is not None:
        scratch.append(pltpu.VMEM((tm, d), F32))
    out = pl.pallas_call(
        functools.partial(_ffn_kernel, tf=tf, blk=blk, pre_proj=pre is not None, post=layouts,
                          final_norm=final_g is not None),
        grid=(batch, tiles),
        in_specs=in_specs,
        out_specs=[row] + [layout_spec[layout] for layout in layouts],
        out_shape=[jax.ShapeDtypeStruct((m, d), F32)]
        + [jax.ShapeDtypeStruct(layout_shape[layout], BF16) for layout in layouts],
        scratch_shapes=scratch,
        compiler_params=_params("parallel", "parallel"),
        name="ffn",
    )(*args)
    return out[0] if not post else out


def _conv_kernel(x_ref, ng_ref, w1_ref, b1_ref, wdw_ref, bdw_ref, lng_ref, lnb_ref, w2_ref, b2_ref,
                 o_ref, gbuf_ref, cbuf_ref, *, tiles_per_seq, width, col_chunk, row_chunk):
    tm, d = x_ref.shape
    i = pl.program_id(0)

    @pl.when(i % tiles_per_seq == 0)
    def _():
        gbuf_ref[0:CONV_HALO, :] = jnp.zeros((CONV_HALO, d), F32)

    @pl.when(i % tiles_per_seq != 0)
    def _():
        gbuf_ref[0:CONV_HALO, :] = gbuf_ref[tm:tm + CONV_HALO, :]

    x = x_ref[...]
    u = _rms(x, ng_ref[...]).astype(BF16)
    for c in range(d // col_chunk):
        lo, hi = c * col_chunk, (c + 1) * col_chunk
        a = jnp.dot(u, w1_ref[:, lo:hi], preferred_element_type=F32) + b1_ref[:, lo:hi]
        gate = jnp.dot(u, w1_ref[:, d + lo:d + hi], preferred_element_type=F32) + b1_ref[:, d + lo:d + hi]
        gbuf_ref[CONV_HALO:CONV_HALO + tm, lo:hi] = a * jax.nn.sigmoid(gate)

    pad = CONV_HALO - (width - 1)

    def conv_rows(i, carry):
        r0 = pl.multiple_of(i * row_chunk, row_chunk)
        for lo in range(0, d, LANES):
            lanes = slice(lo, lo + LANES)
            acc = jnp.broadcast_to(bdw_ref[:, lanes], (row_chunk, LANES))
            for b in range(SUBLANES):
                rows = row_chunk if b == 0 else row_chunk + SUBLANES
                part = None
                for a in range((pad + width - 1) // SUBLANES + 1):
                    w = SUBLANES * a + b - pad
                    if 0 <= w < width:
                        term = gbuf_ref[pl.ds(r0 + SUBLANES * a, rows), lanes] * wdw_ref[w:w + 1, lanes]
                        part = term if part is None else part + term
                acc = acc + part[b:b + row_chunk]
            cbuf_ref[pl.ds(r0, row_chunk), lanes] = acc
        return carry

    lax.fori_loop(0, tm // row_chunk, conv_rows, 0)

    hc = cbuf_ref[...]
    mu = jnp.mean(hc, axis=-1, keepdims=True)
    xc = hc - mu
    var = jnp.mean(xc * xc, axis=-1, keepdims=True)
    y = xc * lax.rsqrt(var + LN_EPS) * lng_ref[...] + lnb_ref[...]
    y = (y * jax.nn.sigmoid(y)).astype(BF16)
    o_ref[...] = x + jnp.dot(y, w2_ref[...], preferred_element_type=F32) + b2_ref[...]


def _conv_module(h, norm_g, w1, b1, wdw, bdw, lng, lnb, w2, b2, *, seq, tm=512):
    m, d = h.shape
    width = wdw.shape[0]
    assert width - 1 <= CONV_HALO and seq % tm == 0 and tm >= 2 * CONV_HALO
    row = pl.BlockSpec((tm, d), lambda i: (i, 0))
    vec = _resident((1, d))
    return pl.pallas_call(
        functools.partial(_conv_kernel, tiles_per_seq=seq // tm, width=width, col_chunk=256, row_chunk=64),
        grid=(m // tm,),
        in_specs=[row, vec, _resident(w1.shape), _resident(b1.shape), _resident(wdw.shape), vec, vec, vec,
                  _resident(w2.shape), vec],
        out_specs=row,
        out_shape=jax.ShapeDtypeStruct((m, d), F32),
        scratch_shapes=[pltpu.VMEM((CONV_HALO + tm, d), F32), pltpu.VMEM((tm, d), F32)],
        compiler_params=_params("arbitrary"),
        name="conv_module",
    )(h, norm_g, w1, b1, wdw, bdw, lng, lnb, w2, b2)


def _attn_kernel(hp_tab, qi_tab, kj_tab, k_ref, qt_ref, vt_ref, u_ref, bias_ref, ot_ref,
                 zbuf, hbuf, tbuf, wbuf, tot_ref, c_ref, acc_ref, *, head_dim, n_pairs):
    blk = u_ref.shape[0]
    q_row = lax.broadcasted_iota(jnp.int32, (LANES, blk), 0)
    c_ref[...] = jnp.zeros_like(c_ref)
    acc_ref[...] = jnp.zeros_like(acc_ref)
    heads = range(HEADS_PER_STEP)

    def scores(p, slot):
        hp, qi, kj = hp_tab[p], qi_tab[p], kj_tab[p]
        kb = k_ref[hp, pl.ds(pl.multiple_of(kj * blk, blk), blk), :]
        qt = qt_ref[hp, qi]
        bias = bias_ref[(qi == kj).astype(jnp.int32)]
        for h in heads:
            qh = jnp.where(q_row // head_dim == h, qt, jnp.zeros_like(qt))
            zbuf[slot, h] = jnp.dot(kb, qh, preferred_element_type=F32) + bias

    def softplus(p, slot):
        for h in heads:
            z = zbuf[slot, h]
            sp = jnp.maximum(z, 0.0) + jnp.log2(1.0 + jnp.exp2(-jnp.abs(z)))
            hbuf[slot, h] = sp.astype(BF16)
            zbuf[slot, h] = z - sp

    def later_key_sums(p, slot):
        for h in heads:
            r = jnp.dot(u_ref[...], hbuf[slot, h], preferred_element_type=F32)
            tbuf[slot, h] = (zbuf[slot, h] - r).astype(BF16)
            tot_ref[slot, h] = r[0:1, :] + hbuf[slot, h, 0:1, :].astype(F32)

    def weights(p, slot):
        for h in heads:
            wbuf[slot, h] = jnp.exp2(tbuf[slot, h])

    def values(p, slot):
        hp, qi, kj = hp_tab[p], qi_tab[p], kj_tab[p]
        first = qi == kj
        for h in heads:
            rows = slice(h * head_dim, (h + 1) * head_dim)
            pv = jnp.dot(vt_ref[hp, kj, rows, :], wbuf[slot, h], preferred_element_type=F32)
            c = jnp.where(first, 0.0, c_ref[h])
            acc = jnp.where(first, 0.0, acc_ref[h]) + pv * jnp.exp2(-c)
            acc_ref[h] = acc
            c_ref[h] = c + tot_ref[slot, h]
            ot_ref[hp, qi, rows, :] = acc.astype(ot_ref.dtype)

    stages = (scores, softplus, later_key_sums, weights, values)

    depth = len(stages)

    def step(p, p_mod, lo, hi):
        for k in reversed(range(lo, hi)):
            stages[k](p - k, (p_mod - k) % PIPE_SLOTS)

    main_start = PIPE_SLOTS * ((depth - 1 + PIPE_SLOTS - 1) // PIPE_SLOTS)
    main_end = n_pairs - n_pairs % PIPE_SLOTS
    assert main_start <= main_end
    for p in range(main_start):
        step(p, p % PIPE_SLOTS, 0, min(p + 1, depth))

    def body(i, carry):
        base = main_start + i * PIPE_SLOTS
        for d in range(PIPE_SLOTS):
            step(base + d, d, 0, depth)
        return carry

    lax.fori_loop(0, (main_end - main_start) // PIPE_SLOTS, body, 0)
    for p in range(main_end, n_pairs + depth - 1):
        step(p, p % PIPE_SLOTS, max(0, p - n_pairs + 1), depth)


def _attention(qt, k, vt):
    batch, n_hp, n_blk, _, blk = qt.shape
    seq = n_blk * blk
    head_dim = LANES // HEADS_PER_STEP
    pairs = [(hp, qi, kj) for hp in range(n_hp) for qi in range(n_blk) for kj in range(qi, -1, -1)]
    tabs = [jnp.asarray(np.array([p[c] for p in pairs], np.int32)) for c in range(3)]

    strictly_causal = (lax.broadcasted_iota(jnp.int32, (blk, blk), 0)
                       < lax.broadcasted_iota(jnp.int32, (blk, blk), 1))
    u = strictly_causal.astype(BF16)
    bias = jnp.stack([jnp.zeros((blk, blk), F32), jnp.where(strictly_causal, 0.0, -jnp.inf).astype(F32)])

    tspec = pl.BlockSpec((None, n_hp, n_blk, LANES, blk), lambda b, *_: (b, 0, 0, 0, 0))
    return pl.pallas_call(
        functools.partial(_attn_kernel, head_dim=head_dim, n_pairs=len(pairs)),
        grid_spec=pltpu.PrefetchScalarGridSpec(
            num_scalar_prefetch=len(tabs),
            grid=(batch,),
            in_specs=[pl.BlockSpec((None, n_hp, seq, LANES), lambda b, *_: (b, 0, 0, 0)), tspec, tspec,
                      _resident(u.shape), _resident(bias.shape)],
            out_specs=tspec,
            scratch_shapes=[
                pltpu.VMEM((PIPE_SLOTS, HEADS_PER_STEP, blk, blk), F32),
                pltpu.VMEM((PIPE_SLOTS, HEADS_PER_STEP, blk, blk), BF16),
                pltpu.VMEM((PIPE_SLOTS, HEADS_PER_STEP, blk, blk), BF16),
                pltpu.VMEM((PIPE_SLOTS, HEADS_PER_STEP, blk, blk), BF16),
                pltpu.VMEM((PIPE_SLOTS, HEADS_PER_STEP, 1, blk), F32),
                pltpu.VMEM((HEADS_PER_STEP, 1, blk), F32),
                pltpu.VMEM((HEADS_PER_STEP, head_dim, blk), F32),
            ]),
        out_shape=jax.ShapeDtypeStruct(qt.shape, BF16),
        compiler_params=_params("parallel"),
        name="stickbreaking_attention",
    )(*tabs, k, qt, vt, u, bias)


def kernel(x, ffn1_norm, ffn1_w_gate, ffn1_w_up, ffn1_w_down, mix_norm, ffn2_norm, ffn2_w_gate, ffn2_w_up, ffn2_w_down, conv_w_pw1, conv_b_pw1, conv_w_dw, conv_b_dw, conv_ln_g, conv_ln_b, conv_w_pw2, conv_b_pw2, kv_norm, w_kv, attn_w_q, attn_w_o, final_norm):
    batch, seq, d = x.shape
    depth = ffn1_norm.shape[0]
    n_conv = conv_w_pw1.shape[0]
    head_dim = d // N_HEADS
    assert head_dim * HEADS_PER_STEP == LANES

    cast = lambda w: w.astype(BF16)
    ffn1 = (cast(ffn1_w_gate), cast(ffn1_w_up), cast(ffn1_w_down))
    ffn2 = (cast(ffn2_w_gate), cast(ffn2_w_up), cast(ffn2_w_down))
    q_scale = math.log2(math.e) * head_dim ** -0.5
    row = lambda v: v.reshape(1, -1)

    geom = dict(batch=batch, seq=seq, blk=256)
    h = x.reshape(batch * seq, d)
    k = vt = None
    for layer in range(depth):
        final_g = row(final_norm) if layer == depth - 1 else None
        if layer < n_conv:
            i = layer
            h = _ffn(h, row(ffn1_norm[layer]), *ffn1, layer, **geom)
            h = _conv_module(h, row(mix_norm[layer]), cast(conv_w_pw1[i]), row(conv_b_pw1[i]), conv_w_dw[i],
                             row(conv_b_dw[i]), row(conv_ln_g[i]), row(conv_ln_b[i]), cast(conv_w_pw2[i]),
                             row(conv_b_pw2[i]), seq=seq)
            if layer == n_conv - 1:
                h, k, vt = _ffn(h, row(ffn2_norm[layer]), *ffn2, layer, final_g=final_g, post_g=row(kv_norm),
                                post=((NATURAL, cast(w_kv[:, :d])), (TRANSPOSED, cast(w_kv[:, d:].T))), **geom)
            else:
                h = _ffn(h, row(ffn2_norm[layer]), *ffn2, layer, final_g=final_g, **geom)
        else:
            i = layer - n_conv
            h, qt = _ffn(h, row(ffn1_norm[layer]), *ffn1, layer, post_g=row(mix_norm[layer]),
                         post=((TRANSPOSED, cast(attn_w_q[i].T * q_scale)),), **geom)
            h = _ffn(h, row(ffn2_norm[layer]), *ffn2, layer, pre=(_attention(qt, k, vt), cast(attn_w_o[i])),
                     final_g=final_g, **geom)
    return h.reshape(batch, seq, d)
```

```python
import functools
import math

import numpy as np
import jax
import jax.numpy as jnp
from jax import lax
from jax.experimental import pallas as pl
from jax.experimental.pallas import tpu as pltpu

RMS_EPS = 1e-6
LN_EPS = 1e-5
HALF_STEP = 0.5
N_HEADS = 16
LANES = 128
SUBLANES = 8
CONV_HALO = 32
HEADS_PER_STEP = 2
PIPE_SLOTS = 8
VMEM_LIMIT = 56 * 1024 * 1024

F32 = jnp.float32
BF16 = jnp.bfloat16


def _rms(xf, g):
    ms = jnp.mean(xf * xf, axis=-1, keepdims=True)
    return xf * lax.rsqrt(ms + RMS_EPS) * g


def _resident(shape):
    zeros = (0,) * len(shape)
    return pl.BlockSpec(shape, lambda *_: zeros, pipeline_mode=pl.Buffered(1))


def _params(*sem):
    return pltpu.CompilerParams(dimension_semantics=sem, vmem_limit_bytes=VMEM_LIMIT)


NATURAL, TRANSPOSED = "natural", "transposed"
NT_DIMS = (((1,), (1,)), ((), ()))


def _ffn_kernel(*refs, tf, blk, pre_proj, post, final_norm, n_convert):
    refs = list(refs)
    x_ref = refs.pop(0)
    if pre_proj:
        ot_ref, wo_ref = refs.pop(0), refs.pop(0)
    g_ref, wg_ref, wu_ref, wd_ref = (refs.pop(0) for _ in range(4))
    if final_norm:
        fg_ref = refs.pop(0)
    if post:
        pg_ref = refs.pop(0)
    post_w = [refs.pop(0) for _ in post]
    cv_in = [refs.pop(0) for _ in range(n_convert)]
    o_ref = refs.pop(0)
    post_out = [refs.pop(0) for _ in post]
    cv_out = [refs.pop(0) for _ in range(n_convert)]
    hm_ref = refs.pop(0)
    for src, dst in zip(cv_in, cv_out):
        dst[...] = src[...].astype(dst.dtype)
    if pre_proj:
        xs_ref = refs.pop(0)
        n_hp = ot_ref.shape[0]
        for j in range(ot_ref.shape[1]):
            rows = slice(j * blk, (j + 1) * blk)
            o = jnp.concatenate([ot_ref[hp, j].astype(F32).T.astype(BF16) for hp in range(n_hp)], axis=1)
            xs_ref[rows, :] = x_ref[rows, :] + jnp.dot(o, wo_ref[...], preferred_element_type=F32)
        x_ref = xs_ref

    xn = _rms(x_ref[...], g_ref[...]).astype(BF16)
    for lo in range(0, hm_ref.shape[1], tf):
        g = jnp.dot(xn, wg_ref[:, lo:lo + tf], preferred_element_type=F32)
        u = jnp.dot(xn, wu_ref[:, lo:lo + tf], preferred_element_type=F32)
        hm_ref[:, lo:lo + tf] = (g * jax.nn.sigmoid(g) * u).astype(BF16)
    y = x_ref[...] + HALF_STEP * jnp.dot(hm_ref[...], wd_ref[...], preferred_element_type=F32)
    if final_norm:
        y = _rms(y, fg_ref[...])
    o_ref[...] = y

    if post:
        yn = _rms(y, pg_ref[...]).astype(BF16)
    for layout, w_ref, out_ref in zip(post, post_w, post_out):
        if layout == NATURAL:
            res = jnp.dot(yn, w_ref[...], preferred_element_type=F32)
            for hp in range(out_ref.shape[0]):
                out_ref[hp] = res[:, hp * LANES:(hp + 1) * LANES].astype(out_ref.dtype)
        else:
            res_t = lax.dot_general(w_ref[...], yn, NT_DIMS, preferred_element_type=F32)
            for hp in range(out_ref.shape[0]):
                for j in range(out_ref.shape[1]):
                    out_ref[hp, j] = res_t[hp * LANES:(hp + 1) * LANES, j * blk:(j + 1) * blk].astype(out_ref.dtype)


CONVERT_COLS = 4 * LANES


def _ffn(h, norm_g, wg, wu, wd, *, batch, seq, blk, pre=None, final_g=None, post_g=None, post=(),
         convert=None, tm=512, tf=256):
    m, d = h.shape
    d_ff = wg.shape[1]
    n_hp = d // LANES
    tiles = seq // tm
    steps = batch * tiles
    row = pl.BlockSpec((tm, d), lambda b, i: (b * tiles + i, 0))
    vec = _resident((1, d))
    layout_spec = {
        NATURAL: pl.BlockSpec((None, n_hp, tm, LANES), lambda b, i: (b, 0, i, 0)),
        TRANSPOSED: pl.BlockSpec((None, n_hp, tm // blk, LANES, blk), lambda b, i: (b, 0, i, 0, 0)),
    }
    layout_shape = {NATURAL: (batch, n_hp, seq, LANES), TRANSPOSED: (batch, n_hp, seq // blk, LANES, blk)}

    in_specs, args = [row], [h]
    if pre is not None:
        in_specs += [layout_spec[TRANSPOSED], _resident(pre[1].shape)]
        args += list(pre)
    in_specs += [vec, _resident(wg.shape), _resident(wu.shape), _resident(wd.shape)]
    args += [norm_g, wg, wu, wd]
    if final_g is not None:
        in_specs.append(vec)
        args.append(final_g)
    if post:
        in_specs.append(vec)
        args.append(post_g)
    in_specs += [_resident(w.shape) for _, w in post]
    args += [w for _, w in post]
    layouts = tuple(layout for layout, _ in post)
    out_specs = [row] + [layout_spec[layout] for layout in layouts]
    out_shape = ([jax.ShapeDtypeStruct((m, d), F32)]
                 + [jax.ShapeDtypeStruct(layout_shape[layout], BF16) for layout in layouts])
    cv_shapes = []
    if convert is not None:
        cv_layer, stacks = convert
        for w in stacks:
            rows, rem = divmod(w[0].size, CONVERT_COLS * steps)
            assert rem == 0 and rows % (2 * SUBLANES) == 0
            flat = w.reshape(w.shape[0], rows * steps, CONVERT_COLS)
            args.append(flat)
            in_specs.append(pl.BlockSpec((None, rows, CONVERT_COLS),
                                         lambda b, i: (cv_layer, b * tiles + i, 0)))
            out_specs.append(pl.BlockSpec((rows, CONVERT_COLS), lambda b, i: (b * tiles + i, 0)))
            out_shape.append(jax.ShapeDtypeStruct(flat.shape[1:], BF16))
            cv_shapes.append(w.shape[1:])
    scratch = [pltpu.VMEM((tm, d_ff), BF16)]
    if pre is not None:
        scratch.append(pltpu.VMEM((tm, d), F32))
    out = pl.pallas_call(
        functools.partial(_ffn_kernel, tf=tf, blk=blk, pre_proj=pre is not None, post=layouts,
                          final_norm=final_g is not None, n_convert=len(cv_shapes)),
        grid=(batch, tiles),
        in_specs=in_specs,
        out_specs=out_specs,
        out_shape=out_shape,
        scratch_shapes=scratch,
        compiler_params=_params("parallel", "parallel"),
        name="ffn",
    )(*args)
    n_main = 1 + len(layouts)
    converted = [o.reshape(s) for o, s in zip(out[n_main:], cv_shapes)]
    return out[0], list(out[1:n_main]), converted


def _conv_kernel(x_ref, ng_ref, w1_ref, b1_ref, wdw_ref, bdw_ref, lng_ref, lnb_ref, w2_ref, b2_ref,
                 o_ref, gbuf_ref, cbuf_ref, *, tiles_per_seq, width, col_chunk, row_chunk):
    tm, d = x_ref.shape
    i = pl.program_id(0)

    @pl.when(i % tiles_per_seq == 0)
    def _():
        gbuf_ref[0:CONV_HALO, :] = jnp.zeros((CONV_HALO, d), F32)

    @pl.when(i % tiles_per_seq != 0)
    def _():
        gbuf_ref[0:CONV_HALO, :] = gbuf_ref[tm:tm + CONV_HALO, :]

    x = x_ref[...]
    u = _rms(x, ng_ref[...]).astype(BF16)
    for c in range(d // col_chunk):
        lo, hi = c * col_chunk, (c + 1) * col_chunk
        a = jnp.dot(u, w1_ref[:, lo:hi], preferred_element_type=F32) + b1_ref[:, lo:hi]
        gate = jnp.dot(u, w1_ref[:, d + lo:d + hi], preferred_element_type=F32) + b1_ref[:, d + lo:d + hi]
        gbuf_ref[CONV_HALO:CONV_HALO + tm, lo:hi] = a * jax.nn.sigmoid(gate)

    pad = CONV_HALO - (width - 1)

    def conv_rows(i, carry):
        r0 = pl.multiple_of(i * row_chunk, row_chunk)
        for lo in range(0, d, LANES):
            lanes = slice(lo, lo + LANES)
            acc = jnp.broadcast_to(bdw_ref[:, lanes], (row_chunk, LANES))
            for b in range(SUBLANES):
                rows = row_chunk if b == 0 else row_chunk + SUBLANES
                part = None
                for a in range((pad + width - 1) // SUBLANES + 1):
                    w = SUBLANES * a + b - pad
                    if 0 <= w < width:
                        term = gbuf_ref[pl.ds(r0 + SUBLANES * a, rows), lanes] * wdw_ref[w:w + 1, lanes]
                        part = term if part is None else part + term
                acc = acc + part[b:b + row_chunk]
            cbuf_ref[pl.ds(r0, row_chunk), lanes] = acc
        return carry

    lax.fori_loop(0, tm // row_chunk, conv_rows, 0)

    hc = cbuf_ref[...]
    mu = jnp.mean(hc, axis=-1, keepdims=True)
    xc = hc - mu
    var = jnp.mean(xc * xc, axis=-1, keepdims=True)
    y = xc * lax.rsqrt(var + LN_EPS) * lng_ref[...] + lnb_ref[...]
    y = (y * jax.nn.sigmoid(y)).astype(BF16)
    o_ref[...] = x + jnp.dot(y, w2_ref[...], preferred_element_type=F32) + b2_ref[...]


def _conv_module(h, norm_g, w1, b1, wdw, bdw, lng, lnb, w2, b2, *, seq, tm=512):
    m, d = h.shape
    width = wdw.shape[0]
    assert width - 1 <= CONV_HALO and seq % tm == 0 and tm >= 2 * CONV_HALO
    row = pl.BlockSpec((tm, d), lambda i: (i, 0))
    vec = _resident((1, d))
    return pl.pallas_call(
        functools.partial(_conv_kernel, tiles_per_seq=seq // tm, width=width, col_chunk=256, row_chunk=64),
        grid=(m // tm,),
        in_specs=[row, vec, _resident(w1.shape), _resident(b1.shape), _resident(wdw.shape), vec, vec, vec,
                  _resident(w2.shape), vec],
        out_specs=row,
        out_shape=jax.ShapeDtypeStruct((m, d), F32),
        scratch_shapes=[pltpu.VMEM((CONV_HALO + tm, d), F32), pltpu.VMEM((tm, d), F32)],
        compiler_params=_params("arbitrary"),
        name="conv_module",
    )(h, norm_g, w1, b1, wdw, bdw, lng, lnb, w2, b2)


def _attn_kernel(hp_tab, qi_tab, kj_tab, k_ref, qt_ref, vt_ref, u_ref, bias_ref, ot_ref,
                 zbuf, hbuf, tbuf, wbuf, tot_ref, c_ref, acc_ref, *, head_dim, n_pairs):
    blk = u_ref.shape[0]
    q_row = lax.broadcasted_iota(jnp.int32, (LANES, blk), 0)
    c_ref[...] = jnp.zeros_like(c_ref)
    acc_ref[...] = jnp.zeros_like(acc_ref)
    heads = range(HEADS_PER_STEP)

    def scores(p, slot):
        hp, qi, kj = hp_tab[p], qi_tab[p], kj_tab[p]
        kb = k_ref[hp, pl.ds(pl.multiple_of(kj * blk, blk), blk), :]
        qt = qt_ref[hp, qi]
        bias = bias_ref[(qi == kj).astype(jnp.int32)]
        for h in heads:
            qh = jnp.where(q_row // head_dim == h, qt, jnp.zeros_like(qt))
            zbuf[slot, h] = jnp.dot(kb, qh, preferred_element_type=F32) + bias

    def softplus(p, slot):
        for h in heads:
            z = zbuf[slot, h]
            sp = jnp.maximum(z, 0.0) + jnp.log2(1.0 + jnp.exp2(-jnp.abs(z)))
            hbuf[slot, h] = sp.astype(BF16)
            zbuf[slot, h] = z - sp

    def later_key_sums(p, slot):
        for h in heads:
            r = jnp.dot(u_ref[...], hbuf[slot, h], preferred_element_type=F32)
            tbuf[slot, h] = (zbuf[slot, h] - r).astype(BF16)
            tot_ref[slot, h] = r[0:1, :] + hbuf[slot, h, 0:1, :].astype(F32)

    def weights(p, slot):
        for h in heads:
            wbuf[slot, h] = jnp.exp2(tbuf[slot, h])

    def values(p, slot):
        hp, qi, kj = hp_tab[p], qi_tab[p], kj_tab[p]
        first = qi == kj
        for h in heads:
            rows = slice(h * head_dim, (h + 1) * head_dim)
            pv = jnp.dot(vt_ref[hp, kj, rows, :], wbuf[slot, h], preferred_element_type=F32)
            c = jnp.where(first, 0.0, c_ref[h])
            acc = jnp.where(first, 0.0, acc_ref[h]) + pv * jnp.exp2(-c)
            acc_ref[h] = acc
            c_ref[h] = c + tot_ref[slot, h]
            ot_ref[hp, qi, rows, :] = acc.astype(ot_ref.dtype)

    stages = (scores, softplus, later_key_sums, weights, values)

    depth = len(stages)

    def step(p, p_mod, lo, hi):
        for k in reversed(range(lo, hi)):
            stages[k](p - k, (p_mod - k) % PIPE_SLOTS)

    main_start = PIPE_SLOTS * ((depth - 1 + PIPE_SLOTS - 1) // PIPE_SLOTS)
    main_end = n_pairs - n_pairs % PIPE_SLOTS
    assert main_start <= main_end
    for p in range(main_start):
        step(p, p % PIPE_SLOTS, 0, min(p + 1, depth))

    def body(i, carry):
        base = main_start + i * PIPE_SLOTS
        for d in range(PIPE_SLOTS):
            step(base + d, d, 0, depth)
        return carry

    lax.fori_loop(0, (main_end - main_start) // PIPE_SLOTS, body, 0)
    for p in range(main_end, n_pairs + depth - 1):
        step(p, p % PIPE_SLOTS, max(0, p - n_pairs + 1), depth)


def _attention(qt, k, vt):
    batch, n_hp, n_blk, _, blk = qt.shape
    seq = n_blk * blk
    head_dim = LANES // HEADS_PER_STEP
    pairs = [(hp, qi, kj) for hp in range(n_hp) for qi in range(n_blk) for kj in range(qi, -1, -1)]
    tabs = [jnp.asarray(np.array([p[c] for p in pairs], np.int32)) for c in range(3)]

    strictly_causal = (lax.broadcasted_iota(jnp.int32, (blk, blk), 0)
                       < lax.broadcasted_iota(jnp.int32, (blk, blk), 1))
    u = strictly_causal.astype(BF16)
    bias = jnp.stack([jnp.zeros((blk, blk), F32), jnp.where(strictly_causal, 0.0, -jnp.inf).astype(F32)])

    tspec = pl.BlockSpec((None, n_hp, n_blk, LANES, blk), lambda b, *_: (b, 0, 0, 0, 0))
    return pl.pallas_call(
        functools.partial(_attn_kernel, head_dim=head_dim, n_pairs=len(pairs)),
        grid_spec=pltpu.PrefetchScalarGridSpec(
            num_scalar_prefetch=len(tabs),
            grid=(batch,),
            in_specs=[pl.BlockSpec((None, n_hp, seq, LANES), lambda b, *_: (b, 0, 0, 0)), tspec, tspec,
                      _resident(u.shape), _resident(bias.shape)],
            out_specs=tspec,
            scratch_shapes=[
                pltpu.VMEM((PIPE_SLOTS, HEADS_PER_STEP, blk, blk), F32),
                pltpu.VMEM((PIPE_SLOTS, HEADS_PER_STEP, blk, blk), BF16),
                pltpu.VMEM((PIPE_SLOTS, HEADS_PER_STEP, blk, blk), BF16),
                pltpu.VMEM((PIPE_SLOTS, HEADS_PER_STEP, blk, blk), BF16),
                pltpu.VMEM((PIPE_SLOTS, HEADS_PER_STEP, 1, blk), F32),
                pltpu.VMEM((HEADS_PER_STEP, 1, blk), F32),
                pltpu.VMEM((HEADS_PER_STEP, head_dim, blk), F32),
            ]),
        out_shape=jax.ShapeDtypeStruct(qt.shape, BF16),
        compiler_params=_params("parallel"),
        name="stickbreaking_attention",
    )(*tabs, k, qt, vt, u, bias)


def kernel(x, ffn1_norm, ffn1_w_gate, ffn1_w_up, ffn1_w_down, mix_norm, ffn2_norm, ffn2_w_gate, ffn2_w_up, ffn2_w_down, conv_w_pw1, conv_b_pw1, conv_w_dw, conv_b_dw, conv_ln_g, conv_ln_b, conv_w_pw2, conv_b_pw2, kv_norm, w_kv, attn_w_q, attn_w_o, final_norm):
    batch, seq, d = x.shape
    depth = ffn1_norm.shape[0]
    n_conv = conv_w_pw1.shape[0]
    head_dim = d // N_HEADS
    assert head_dim * HEADS_PER_STEP == LANES

    cast = lambda w: w.astype(BF16)
    q_scale = math.log2(math.e) * head_dim ** -0.5
    row = lambda v: v.reshape(1, -1)

    ffn_stacks = {1: [ffn1_w_gate, ffn1_w_up, ffn1_w_down], 2: [ffn2_w_gate, ffn2_w_up, ffn2_w_down]}
    ffn_norms = {1: ffn1_norm, 2: ffn2_norm}
    calls = [(which, layer) for layer in range(depth) for which in (1, 2)]
    state = {"w": [cast(w[0]) for w in ffn_stacks[1]], "n": 0}
    geom = dict(batch=batch, seq=seq, blk=256)

    def ffn(h, **kw):
        which, layer = calls[state["n"]]
        state["n"] += 1
        nxt = calls[state["n"]] if state["n"] < len(calls) else None
        convert = None if nxt is None else (nxt[1], ffn_stacks[nxt[0]])
        y, posts, state["w"] = _ffn(h, row(ffn_norms[which][layer]), *state["w"], convert=convert, **kw, **geom)
        return (y, *posts) if posts else y

    h = x.reshape(batch * seq, d)
    k = vt = None
    for layer in range(depth):
        final_g = row(final_norm) if layer == depth - 1 else None
        if layer < n_conv:
            i = layer
            h = ffn(h)
            h = _conv_module(h, row(mix_norm[layer]), cast(conv_w_pw1[i]), row(conv_b_pw1[i]), conv_w_dw[i],
                             row(conv_b_dw[i]), row(conv_ln_g[i]), row(conv_ln_b[i]), cast(conv_w_pw2[i]),
                             row(conv_b_pw2[i]), seq=seq)
            if layer == n_conv - 1:
                h, k, vt = ffn(h, final_g=final_g, post_g=row(kv_norm),
                               post=((NATURAL, cast(w_kv[:, :d])), (TRANSPOSED, cast(w_kv[:, d:].T))))
            else:
                h = ffn(h, final_g=final_g)
        else:
            i = layer - n_conv
            h, qt = ffn(h, post_g=row(mix_norm[layer]), post=((TRANSPOSED, cast(attn_w_q[i].T * q_scale)),))
            h = ffn(h, pre=(_attention(qt, k, vt), cast(attn_w_o[i])), final_g=final_g)
    return h.reshape(batch, seq, d)
```

```python
import functools
import math

import numpy as np
import jax
import jax.numpy as jnp
from jax import lax
from jax.experimental import pallas as pl
from jax.experimental.pallas import tpu as pltpu

RMS_EPS = 1e-6
LN_EPS = 1e-5
HALF_STEP = 0.5
N_HEADS = 16
LANES = 128
SUBLANES = 8
CONV_HALO = 32
HEADS_PER_STEP = 2
PIPE_SLOTS = 8
VMEM_LIMIT = 56 * 1024 * 1024

F32 = jnp.float32
BF16 = jnp.bfloat16


def _rms(xf, g):
    ms = jnp.mean(xf * xf, axis=-1, keepdims=True)
    return xf * lax.rsqrt(ms + RMS_EPS) * g


def _resident(shape):
    zeros = (0,) * len(shape)
    return pl.BlockSpec(shape, lambda *_: zeros, pipeline_mode=pl.Buffered(1))


def _params(*sem):
    return pltpu.CompilerParams(dimension_semantics=sem, vmem_limit_bytes=VMEM_LIMIT)


NATURAL, TRANSPOSED = "natural", "transposed"
NT_DIMS = (((1,), (1,)), ((), ()))


def _ffn_kernel(*refs, tf, blk, pre_proj, post, final_norm, n_convert):
    refs = list(refs)
    x_ref = refs.pop(0)
    if pre_proj:
        ot_ref, wo_ref = refs.pop(0), refs.pop(0)
    g_ref, wg_ref, wu_ref, wd_ref = (refs.pop(0) for _ in range(4))
    if final_norm:
        fg_ref = refs.pop(0)
    if post:
        pg_ref = refs.pop(0)
    post_w = [refs.pop(0) for _ in post]
    cv_in = [refs.pop(0) for _ in range(n_convert)]
    o_ref = refs.pop(0)
    post_out = [refs.pop(0) for _ in post]
    cv_out = [refs.pop(0) for _ in range(n_convert)]
    hm_ref = refs.pop(0)
    for src, dst in zip(cv_in, cv_out):
        dst[...] = src[...].astype(dst.dtype)
    if pre_proj:
        xs_ref = refs.pop(0)
        n_hp = ot_ref.shape[0]
        for j in range(ot_ref.shape[1]):
            rows = slice(j * blk, (j + 1) * blk)
            o = jnp.concatenate([ot_ref[hp, j].astype(F32).T.astype(BF16) for hp in range(n_hp)], axis=1)
            xs_ref[rows, :] = x_ref[rows, :] + jnp.dot(o, wo_ref[...], preferred_element_type=F32)
        x_ref = xs_ref

    xn = _rms(x_ref[...], g_ref[...]).astype(BF16)
    for lo in range(0, hm_ref.shape[1], tf):
        g = jnp.dot(xn, wg_ref[:, lo:lo + tf], preferred_element_type=F32)
        u = jnp.dot(xn, wu_ref[:, lo:lo + tf], preferred_element_type=F32)
        hm_ref[:, lo:lo + tf] = (g * jax.nn.sigmoid(g) * u).astype(BF16)
    y = x_ref[...] + HALF_STEP * jnp.dot(hm_ref[...], wd_ref[...], preferred_element_type=F32)
    if final_norm:
        y = _rms(y, fg_ref[...])
    o_ref[...] = y

    if post:
        yn = _rms(y, pg_ref[...]).astype(BF16)
    for layout, w_ref, out_ref in zip(post, post_w, post_out):
        if layout == NATURAL:
            res = jnp.dot(yn, w_ref[...], preferred_element_type=F32)
            for hp in range(out_ref.shape[0]):
                out_ref[hp] = res[:, hp * LANES:(hp + 1) * LANES].astype(out_ref.dtype)
        else:
            res_t = lax.dot_general(w_ref[...], yn, NT_DIMS, preferred_element_type=F32)
            for hp in range(out_ref.shape[0]):
                for j in range(out_ref.shape[1]):
                    out_ref[hp, j] = res_t[hp * LANES:(hp + 1) * LANES, j * blk:(j + 1) * blk].astype(out_ref.dtype)


BF16_ROWS = 2 * SUBLANES


def _ffn(h, norm_g, wg, wu, wd, *, batch, seq, blk, pre=None, final_g=None, post_g=None, post=(),
         convert=None, tm=512, tf=256):
    m, d = h.shape
    d_ff = wg.shape[1]
    n_hp = d // LANES
    tiles = seq // tm
    steps = batch * tiles
    row = pl.BlockSpec((tm, d), lambda b, i: (b * tiles + i, 0))
    vec = _resident((1, d))
    layout_spec = {
        NATURAL: pl.BlockSpec((None, n_hp, tm, LANES), lambda b, i: (b, 0, i, 0)),
        TRANSPOSED: pl.BlockSpec((None, n_hp, tm // blk, LANES, blk), lambda b, i: (b, 0, i, 0, 0)),
    }
    layout_shape = {NATURAL: (batch, n_hp, seq, LANES), TRANSPOSED: (batch, n_hp, seq // blk, LANES, blk)}

    in_specs, args = [row], [h]
    if pre is not None:
        in_specs += [layout_spec[TRANSPOSED], _resident(pre[1].shape)]
        args += list(pre)
    in_specs += [vec, _resident(wg.shape), _resident(wu.shape), _resident(wd.shape)]
    args += [norm_g, wg, wu, wd]
    if final_g is not None:
        in_specs.append(vec)
        args.append(final_g)
    if post:
        in_specs.append(vec)
        args.append(post_g)
    in_specs += [_resident(w.shape) for _, w in post]
    args += [w for _, w in post]
    layouts = tuple(layout for layout, _ in post)
    out_specs = [row] + [layout_spec[layout] for layout in layouts]
    out_shape = ([jax.ShapeDtypeStruct((m, d), F32)]
                 + [jax.ShapeDtypeStruct(layout_shape[layout], BF16) for layout in layouts])
    cv_shapes = []
    if convert is not None:
        cv_layer, stacks = convert
        for w in stacks:
            n_rows, n_cols = w.shape[1:]
            share = next(s for s in (1, 2, 4, 8) if n_rows % (steps // s * BF16_ROWS) == 0)
            rows = n_rows // (steps // share)
            args.append(w)
            in_specs.append(pl.BlockSpec((None, rows, n_cols),
                                         lambda b, i, share=share: (cv_layer, (b * tiles + i) // share, 0)))
            out_specs.append(pl.BlockSpec((rows, n_cols), lambda b, i, share=share: ((b * tiles + i) // share, 0)))
            out_shape.append(jax.ShapeDtypeStruct((n_rows, n_cols), BF16))
            cv_shapes.append((n_rows, n_cols))
    scratch = [pltpu.VMEM((tm, d_ff), BF16)]
    if pre is not None:
        scratch.append(pltpu.VMEM((tm, d), F32))
    out = pl.pallas_call(
        functools.partial(_ffn_kernel, tf=tf, blk=blk, pre_proj=pre is not None, post=layouts,
                          final_norm=final_g is not None, n_convert=len(cv_shapes)),
        grid=(batch, tiles),
        in_specs=in_specs,
        out_specs=out_specs,
        out_shape=out_shape,
        scratch_shapes=scratch,
        compiler_params=_params("parallel", "parallel"),
        name="ffn",
    )(*args)
    n_main = 1 + len(layouts)
    converted = [o.reshape(s) for o, s in zip(out[n_main:], cv_shapes)]
    return out[0], list(out[1:n_main]), converted


def _conv_kernel(x_ref, ng_ref, w1_ref, b1_ref, wdw_ref, bdw_ref, lng_ref, lnb_ref, w2_ref, b2_ref,
                 o_ref, gbuf_ref, cbuf_ref, *, tiles_per_seq, width, col_chunk, row_chunk):
    tm, d = x_ref.shape
    i = pl.program_id(0)

    @pl.when(i % tiles_per_seq == 0)
    def _():
        gbuf_ref[0:CONV_HALO, :] = jnp.zeros((CONV_HALO, d), F32)

    @pl.when(i % tiles_per_seq != 0)
    def _():
        gbuf_ref[0:CONV_HALO, :] = gbuf_ref[tm:tm + CONV_HALO, :]

    x = x_ref[...]
    u = _rms(x, ng_ref[...]).astype(BF16)
    for c in range(d // col_chunk):
        lo, hi = c * col_chunk, (c + 1) * col_chunk
        a = jnp.dot(u, w1_ref[:, lo:hi], preferred_element_type=F32) + b1_ref[:, lo:hi]
        gate = jnp.dot(u, w1_ref[:, d + lo:d + hi], preferred_element_type=F32) + b1_ref[:, d + lo:d + hi]
        gbuf_ref[CONV_HALO:CONV_HALO + tm, lo:hi] = a * jax.nn.sigmoid(gate)

    pad = CONV_HALO - (width - 1)

    def conv_rows(i, carry):
        r0 = pl.multiple_of(i * row_chunk, row_chunk)
        for lo in range(0, d, LANES):
            lanes = slice(lo, lo + LANES)
            acc = jnp.broadcast_to(bdw_ref[:, lanes], (row_chunk, LANES))
            for b in range(SUBLANES):
                rows = row_chunk if b == 0 else row_chunk + SUBLANES
                part = None
                for a in range((pad + width - 1) // SUBLANES + 1):
                    w = SUBLANES * a + b - pad
                    if 0 <= w < width:
                        term = gbuf_ref[pl.ds(r0 + SUBLANES * a, rows), lanes] * wdw_ref[w:w + 1, lanes]
                        part = term if part is None else part + term
                acc = acc + part[b:b + row_chunk]
            cbuf_ref[pl.ds(r0, row_chunk), lanes] = acc
        return carry

    lax.fori_loop(0, tm // row_chunk, conv_rows, 0)

    hc = cbuf_ref[...]
    mu = jnp.mean(hc, axis=-1, keepdims=True)
    xc = hc - mu
    var = jnp.mean(xc * xc, axis=-1, keepdims=True)
    y = xc * lax.rsqrt(var + LN_EPS) * lng_ref[...] + lnb_ref[...]
    y = (y * jax.nn.sigmoid(y)).astype(BF16)
    o_ref[...] = x + jnp.dot(y, w2_ref[...], preferred_element_type=F32) + b2_ref[...]


def _conv_module(h, norm_g, w1, b1, wdw, bdw, lng, lnb, w2, b2, *, seq, tm=512):
    m, d = h.shape
    width = wdw.shape[0]
    assert width - 1 <= CONV_HALO and seq % tm == 0 and tm >= 2 * CONV_HALO
    row = pl.BlockSpec((tm, d), lambda i: (i, 0))
    vec = _resident((1, d))
    return pl.pallas_call(
        functools.partial(_conv_kernel, tiles_per_seq=seq // tm, width=width, col_chunk=256, row_chunk=64),
        grid=(m // tm,),
        in_specs=[row, vec, _resident(w1.shape), _resident(b1.shape), _resident(wdw.shape), vec, vec, vec,
                  _resident(w2.shape), vec],
        out_specs=row,
        out_shape=jax.ShapeDtypeStruct((m, d), F32),
        scratch_shapes=[pltpu.VMEM((CONV_HALO + tm, d), F32), pltpu.VMEM((tm, d), F32)],
        compiler_params=_params("arbitrary"),
        name="conv_module",
    )(h, norm_g, w1, b1, wdw, bdw, lng, lnb, w2, b2)


def _attn_kernel(hp_tab, qi_tab, kj_tab, k_ref, qt_ref, vt_ref, u_ref, bias_ref, ot_ref,
                 zbuf, hbuf, tbuf, wbuf, tot_ref, c_ref, acc_ref, *, head_dim, n_pairs):
    blk = u_ref.shape[0]
    q_row = lax.broadcasted_iota(jnp.int32, (LANES, blk), 0)
    c_ref[...] = jnp.zeros_like(c_ref)
    acc_ref[...] = jnp.zeros_like(acc_ref)
    heads = range(HEADS_PER_STEP)

    def scores(p, slot):
        hp, qi, kj = hp_tab[p], qi_tab[p], kj_tab[p]
        kb = k_ref[hp, pl.ds(pl.multiple_of(kj * blk, blk), blk), :]
        qt = qt_ref[hp, qi]
        bias = bias_ref[(qi == kj).astype(jnp.int32)]
        for h in heads:
            qh = jnp.where(q_row // head_dim == h, qt, jnp.zeros_like(qt))
            zbuf[slot, h] = jnp.dot(kb, qh, preferred_element_type=F32) + bias

    def softplus(p, slot):
        for h in heads:
            z = zbuf[slot, h]
            sp = jnp.maximum(z, 0.0) + jnp.log2(1.0 + jnp.exp2(-jnp.abs(z)))
            hbuf[slot, h] = sp.astype(BF16)
            zbuf[slot, h] = z - sp

    def later_key_sums(p, slot):
        for h in heads:
            r = jnp.dot(u_ref[...], hbuf[slot, h], preferred_element_type=F32)
            tbuf[slot, h] = (zbuf[slot, h] - r).astype(BF16)
            tot_ref[slot, h] = r[0:1, :] + hbuf[slot, h, 0:1, :].astype(F32)

    def weights(p, slot):
        for h in heads:
            wbuf[slot, h] = jnp.exp2(tbuf[slot, h])

    def values(p, slot):
        hp, qi, kj = hp_tab[p], qi_tab[p], kj_tab[p]
        first = qi == kj
        for h in heads:
            rows = slice(h * head_dim, (h + 1) * head_dim)
            pv = jnp.dot(vt_ref[hp, kj, rows, :], wbuf[slot, h], preferred_element_type=F32)
            c = jnp.where(first, 0.0, c_ref[h])
            acc = jnp.where(first, 0.0, acc_ref[h]) + pv * jnp.exp2(-c)
            acc_ref[h] = acc
            c_ref[h] = c + tot_ref[slot, h]
            ot_ref[hp, qi, rows, :] = acc.astype(ot_ref.dtype)

    stages = (scores, softplus, later_key_sums, weights, values)

    depth = len(stages)

    def step(p, p_mod, lo, hi):
        for k in reversed(range(lo, hi)):
            stages[k](p - k, (p_mod - k) % PIPE_SLOTS)

    main_start = PIPE_SLOTS * ((depth - 1 + PIPE_SLOTS - 1) // PIPE_SLOTS)
    main_end = n_pairs - n_pairs % PIPE_SLOTS
    assert main_start <= main_end
    for p in range(main_start):
        step(p, p % PIPE_SLOTS, 0, min(p + 1, depth))

    def body(i, carry):
        base = main_start + i * PIPE_SLOTS
        for d in range(PIPE_SLOTS):
            step(base + d, d, 0, depth)
        return carry

    lax.fori_loop(0, (main_end - main_start) // PIPE_SLOTS, body, 0)
    for p in range(main_end, n_pairs + depth - 1):
        step(p, p % PIPE_SLOTS, max(0, p - n_pairs + 1), depth)


def _attention(qt, k, vt):
    batch, n_hp, n_blk, _, blk = qt.shape
    seq = n_blk * blk
    head_dim = LANES // HEADS_PER_STEP
    pairs = [(hp, qi, kj) for hp in range(n_hp) for qi in range(n_blk) for kj in range(qi, -1, -1)]
    tabs = [jnp.asarray(np.array([p[c] for p in pairs], np.int32)) for c in range(3)]

    strictly_causal = (lax.broadcasted_iota(jnp.int32, (blk, blk), 0)
                       < lax.broadcasted_iota(jnp.int32, (blk, blk), 1))
    u = strictly_causal.astype(BF16)
    bias = jnp.stack([jnp.zeros((blk, blk), F32), jnp.where(strictly_causal, 0.0, -jnp.inf).astype(F32)])

    tspec = pl.BlockSpec((None, n_hp, n_blk, LANES, blk), lambda b, *_: (b, 0, 0, 0, 0))
    return pl.pallas_call(
        functools.partial(_attn_kernel, head_dim=head_dim, n_pairs=len(pairs)),
        grid_spec=pltpu.PrefetchScalarGridSpec(
            num_scalar_prefetch=len(tabs),
            grid=(batch,),
            in_specs=[pl.BlockSpec((None, n_hp, seq, LANES), lambda b, *_: (b, 0, 0, 0)), tspec, tspec,
                      _resident(u.shape), _resident(bias.shape)],
            out_specs=tspec,
            scratch_shapes=[
                pltpu.VMEM((PIPE_SLOTS, HEADS_PER_STEP, blk, blk), F32),
                pltpu.VMEM((PIPE_SLOTS, HEADS_PER_STEP, blk, blk), BF16),
                pltpu.VMEM((PIPE_SLOTS, HEADS_PER_STEP, blk, blk), BF16),
                pltpu.VMEM((PIPE_SLOTS, HEADS_PER_STEP, blk, blk), BF16),
                pltpu.VMEM((PIPE_SLOTS, HEADS_PER_STEP, 1, blk), F32),
                pltpu.VMEM((HEADS_PER_STEP, 1, blk), F32),
                pltpu.VMEM((HEADS_PER_STEP, head_dim, blk), F32),
            ]),
        out_shape=jax.ShapeDtypeStruct(qt.shape, BF16),
        compiler_params=_params("parallel"),
        name="stickbreaking_attention",
    )(*tabs, k, qt, vt, u, bias)


def kernel(x, ffn1_norm, ffn1_w_gate, ffn1_w_up, ffn1_w_down, mix_norm, ffn2_norm, ffn2_w_gate, ffn2_w_up, ffn2_w_down, conv_w_pw1, conv_b_pw1, conv_w_dw, conv_b_dw, conv_ln_g, conv_ln_b, conv_w_pw2, conv_b_pw2, kv_norm, w_kv, attn_w_q, attn_w_o, final_norm):
    batch, seq, d = x.shape
    depth = ffn1_norm.shape[0]
    n_conv = conv_w_pw1.shape[0]
    head_dim = d // N_HEADS
    assert head_dim * HEADS_PER_STEP == LANES

    cast = lambda w: w.astype(BF16)
    q_scale = math.log2(math.e) * head_dim ** -0.5
    row = lambda v: v.reshape(1, -1)

    ffn_stacks = {1: [ffn1_w_gate, ffn1_w_up, ffn1_w_down], 2: [ffn2_w_gate, ffn2_w_up, ffn2_w_down]}
    ffn_norms = {1: ffn1_norm, 2: ffn2_norm}
    calls = [(which, layer) for layer in range(depth) for which in (1, 2)]
    state = {"w": [cast(w[0]) for w in ffn_stacks[1]], "n": 0}
    geom = dict(batch=batch, seq=seq, blk=256)

    def ffn(h, **kw):
        which, layer = calls[state["n"]]
        state["n"] += 1
        nxt = calls[state["n"]] if state["n"] < len(calls) else None
        convert = None if nxt is None else (nxt[1], ffn_stacks[nxt[0]])
        y, posts, state["w"] = _ffn(h, row(ffn_norms[which][layer]), *state["w"], convert=convert, **kw, **geom)
        return (y, *posts) if posts else y

    h = x.reshape(batch * seq, d)
    k = vt = None
    for layer in range(depth):
        final_g = row(final_norm) if layer == depth - 1 else None
        if layer < n_conv:
            i = layer
            h = ffn(h)
            h = _conv_module(h, row(mix_norm[layer]), cast(conv_w_pw1[i]), row(conv_b_pw1[i]), conv_w_dw[i],
                             row(conv_b_dw[i]), row(conv_ln_g[i]), row(conv_ln_b[i]), cast(conv_w_pw2[i]),
                             row(conv_b_pw2[i]), seq=seq)
            if layer == n_conv - 1:
                h, k, vt = ffn(h, final_g=final_g, post_g=row(kv_norm),
                               post=((NATURAL, cast(w_kv[:, :d])), (TRANSPOSED, cast(w_kv[:, d:].T))))
            else:
                h = ffn(h, final_g=final_g)
        else:
            i = layer - n_conv
            h, qt = ffn(h, post_g=row(mix_norm[layer]), post=((TRANSPOSED, cast(attn_w_q[i].T * q_scale)),))
            h = ffn(h, pre=(_attention(qt, k, vt), cast(attn_w_o[i])), final_g=final_g)
    return h.reshape(batch, seq, d)
```

```python
import functools
import math

import numpy as np
import jax
import jax.numpy as jnp
from jax import lax
from jax.experimental import pallas as pl
from jax.experimental.pallas import tpu as pltpu

RMS_EPS = 1e-6
LN_EPS = 1e-5
HALF_STEP = 0.5
N_HEADS = 16
LANES = 128
SUBLANES = 8
CONV_HALO = 32
HEADS_PER_STEP = 2
PIPE_SLOTS = 12
VMEM_LIMIT = 56 * 1024 * 1024

F32 = jnp.float32
BF16 = jnp.bfloat16


def _rms(xf, g):
    ms = jnp.mean(xf * xf, axis=-1, keepdims=True)
    return xf * lax.rsqrt(ms + RMS_EPS) * g


def _resident(shape):
    zeros = (0,) * len(shape)
    return pl.BlockSpec(shape, lambda *_: zeros, pipeline_mode=pl.Buffered(1))


def _params(*sem):
    return pltpu.CompilerParams(dimension_semantics=sem, vmem_limit_bytes=VMEM_LIMIT)


NATURAL, TRANSPOSED = "natural", "transposed"
NT_DIMS = (((1,), (1,)), ((), ()))


def _ffn_kernel(*refs, tf, blk, pre_proj, post, final_norm, n_convert):
    refs = list(refs)
    x_ref = refs.pop(0)
    if pre_proj:
        ot_ref, wo_ref = refs.pop(0), refs.pop(0)
    g_ref, wg_ref, wu_ref, wd_ref = (refs.pop(0) for _ in range(4))
    if final_norm:
        fg_ref = refs.pop(0)
    if post:
        pg_ref = refs.pop(0)
    post_w = [refs.pop(0) for _ in post]
    cv_in = [refs.pop(0) for _ in range(n_convert)]
    o_ref = refs.pop(0)
    post_out = [refs.pop(0) for _ in post]
    cv_out = [refs.pop(0) for _ in range(n_convert)]
    hm_ref = refs.pop(0)
    for src, dst in zip(cv_in, cv_out):
        dst[...] = src[...].astype(dst.dtype)
    if pre_proj:
        xs_ref = refs.pop(0)
        n_hp = ot_ref.shape[0]
        for j in range(ot_ref.shape[1]):
            rows = slice(j * blk, (j + 1) * blk)
            o = jnp.concatenate([ot_ref[hp, j].astype(F32).T.astype(BF16) for hp in range(n_hp)], axis=1)
            xs_ref[rows, :] = x_ref[rows, :] + jnp.dot(o, wo_ref[...], preferred_element_type=F32)
        x_ref = xs_ref

    xn = _rms(x_ref[...], g_ref[...]).astype(BF16)
    for lo in range(0, hm_ref.shape[1], tf):
        g = jnp.dot(xn, wg_ref[:, lo:lo + tf], preferred_element_type=F32)
        u = jnp.dot(xn, wu_ref[:, lo:lo + tf], preferred_element_type=F32)
        hm_ref[:, lo:lo + tf] = (g * jax.nn.sigmoid(g) * u).astype(BF16)
    y = x_ref[...] + HALF_STEP * jnp.dot(hm_ref[...], wd_ref[...], preferred_element_type=F32)
    if final_norm:
        y = _rms(y, fg_ref[...])
    o_ref[...] = y

    if post:
        yn = _rms(y, pg_ref[...]).astype(BF16)
    for layout, w_ref, out_ref in zip(post, post_w, post_out):
        if layout == NATURAL:
            res = jnp.dot(yn, w_ref[...], preferred_element_type=F32)
            for hp in range(out_ref.shape[0]):
                out_ref[hp] = res[:, hp * LANES:(hp + 1) * LANES].astype(out_ref.dtype)
        else:
            res_t = lax.dot_general(w_ref[...], yn, NT_DIMS, preferred_element_type=F32)
            for hp in range(out_ref.shape[0]):
                for j in range(out_ref.shape[1]):
                    out_ref[hp, j] = res_t[hp * LANES:(hp + 1) * LANES, j * blk:(j + 1) * blk].astype(out_ref.dtype)


BF16_ROWS = 2 * SUBLANES


def _ffn(h, norm_g, wg, wu, wd, *, batch, seq, blk, pre=None, final_g=None, post_g=None, post=(),
         convert=None, tm=512, tf=256):
    m, d = h.shape
    d_ff = wg.shape[1]
    n_hp = d // LANES
    tiles = seq // tm
    steps = batch * tiles
    row = pl.BlockSpec((tm, d), lambda b, i: (b * tiles + i, 0))
    vec = _resident((1, d))
    layout_spec = {
        NATURAL: pl.BlockSpec((None, n_hp, tm, LANES), lambda b, i: (b, 0, i, 0)),
        TRANSPOSED: pl.BlockSpec((None, n_hp, tm // blk, LANES, blk), lambda b, i: (b, 0, i, 0, 0)),
    }
    layout_shape = {NATURAL: (batch, n_hp, seq, LANES), TRANSPOSED: (batch, n_hp, seq // blk, LANES, blk)}

    in_specs, args = [row], [h]
    if pre is not None:
        in_specs += [layout_spec[TRANSPOSED], _resident(pre[1].shape)]
        args += list(pre)
    in_specs += [vec, _resident(wg.shape), _resident(wu.shape), _resident(wd.shape)]
    args += [norm_g, wg, wu, wd]
    if final_g is not None:
        in_specs.append(vec)
        args.append(final_g)
    if post:
        in_specs.append(vec)
        args.append(post_g)
    in_specs += [_resident(w.shape) for _, w in post]
    args += [w for _, w in post]
    layouts = tuple(layout for layout, _ in post)
    out_specs = [row] + [layout_spec[layout] for layout in layouts]
    out_shape = ([jax.ShapeDtypeStruct((m, d), F32)]
                 + [jax.ShapeDtypeStruct(layout_shape[layout], BF16) for layout in layouts])
    cv_shapes = []
    if convert is not None:
        cv_layer, stacks = convert
        for w in stacks:
            n_rows, n_cols = w.shape[1:]
            share = next(s for s in (1, 2, 4, 8) if n_rows % (steps // s * BF16_ROWS) == 0)
            rows = n_rows // (steps // share)
            args.append(w)
            in_specs.append(pl.BlockSpec((None, rows, n_cols),
                                         lambda b, i, share=share: (cv_layer, (b * tiles + i) // share, 0)))
            out_specs.append(pl.BlockSpec((rows, n_cols), lambda b, i, share=share: ((b * tiles + i) // share, 0)))
            out_shape.append(jax.ShapeDtypeStruct((n_rows, n_cols), BF16))
            cv_shapes.append((n_rows, n_cols))
    scratch = [pltpu.VMEM((tm, d_ff), BF16)]
    if pre is not None:
        scratch.append(pltpu.VMEM((tm, d), F32))
    out = pl.pallas_call(
        functools.partial(_ffn_kernel, tf=tf, blk=blk, pre_proj=pre is not None, post=layouts,
                          final_norm=final_g is not None, n_convert=len(cv_shapes)),
        grid=(batch, tiles),
        in_specs=in_specs,
        out_specs=out_specs,
        out_shape=out_shape,
        scratch_shapes=scratch,
        compiler_params=_params("parallel", "parallel"),
        name="ffn",
    )(*args)
    n_main = 1 + len(layouts)
    converted = [o.reshape(s) for o, s in zip(out[n_main:], cv_shapes)]
    return out[0], list(out[1:n_main]), converted


def _conv_kernel(x_ref, ng_ref, w1_ref, b1_ref, wdw_ref, bdw_ref, lng_ref, lnb_ref, w2_ref, b2_ref,
                 o_ref, gbuf_ref, cbuf_ref, *, tiles_per_seq, width, col_chunk, row_chunk):
    tm, d = x_ref.shape
    i = pl.program_id(0)

    @pl.when(i % tiles_per_seq == 0)
    def _():
        gbuf_ref[0:CONV_HALO, :] = jnp.zeros((CONV_HALO, d), F32)

    @pl.when(i % tiles_per_seq != 0)
    def _():
        gbuf_ref[0:CONV_HALO, :] = gbuf_ref[tm:tm + CONV_HALO, :]

    x = x_ref[...]
    u = _rms(x, ng_ref[...]).astype(BF16)
    for c in range(d // col_chunk):
        lo, hi = c * col_chunk, (c + 1) * col_chunk
        a = jnp.dot(u, w1_ref[:, lo:hi], preferred_element_type=F32) + b1_ref[:, lo:hi]
        gate = jnp.dot(u, w1_ref[:, d + lo:d + hi], preferred_element_type=F32) + b1_ref[:, d + lo:d + hi]
        gbuf_ref[CONV_HALO:CONV_HALO + tm, lo:hi] = a * jax.nn.sigmoid(gate)

    pad = CONV_HALO - (width - 1)

    def conv_rows(i, carry):
        r0 = pl.multiple_of(i * row_chunk, row_chunk)
        for lo in range(0, d, LANES):
            lanes = slice(lo, lo + LANES)
            acc = jnp.broadcast_to(bdw_ref[:, lanes], (row_chunk, LANES))
            for b in range(SUBLANES):
                rows = row_chunk if b == 0 else row_chunk + SUBLANES
                part = None
                for a in range((pad + width - 1) // SUBLANES + 1):
                    w = SUBLANES * a + b - pad
                    if 0 <= w < width:
                        term = gbuf_ref[pl.ds(r0 + SUBLANES * a, rows), lanes] * wdw_ref[w:w + 1, lanes]
                        part = term if part is None else part + term
                acc = acc + part[b:b + row_chunk]
            cbuf_ref[pl.ds(r0, row_chunk), lanes] = acc
        return carry

    lax.fori_loop(0, tm // row_chunk, conv_rows, 0)

    hc = cbuf_ref[...]
    mu = jnp.mean(hc, axis=-1, keepdims=True)
    xc = hc - mu
    var = jnp.mean(xc * xc, axis=-1, keepdims=True)
    y = xc * lax.rsqrt(var + LN_EPS) * lng_ref[...] + lnb_ref[...]
    y = (y * jax.nn.sigmoid(y)).astype(BF16)
    o_ref[...] = x + jnp.dot(y, w2_ref[...], preferred_element_type=F32) + b2_ref[...]


def _conv_module(h, norm_g, w1, b1, wdw, bdw, lng, lnb, w2, b2, *, seq, tm=512):
    m, d = h.shape
    width = wdw.shape[0]
    assert width - 1 <= CONV_HALO and seq % tm == 0 and tm >= 2 * CONV_HALO
    row = pl.BlockSpec((tm, d), lambda i: (i, 0))
    vec = _resident((1, d))
    return pl.pallas_call(
        functools.partial(_conv_kernel, tiles_per_seq=seq // tm, width=width, col_chunk=256, row_chunk=128),
        grid=(m // tm,),
        in_specs=[row, vec, _resident(w1.shape), _resident(b1.shape), _resident(wdw.shape), vec, vec, vec,
                  _resident(w2.shape), vec],
        out_specs=row,
        out_shape=jax.ShapeDtypeStruct((m, d), F32),
        scratch_shapes=[pltpu.VMEM((CONV_HALO + tm, d), F32), pltpu.VMEM((tm, d), F32)],
        compiler_params=_params("arbitrary"),
        name="conv_module",
    )(h, norm_g, w1, b1, wdw, bdw, lng, lnb, w2, b2)


def _attn_kernel(hp_tab, qi_tab, kj_tab, k_ref, qt_ref, vt_ref, u_ref, bias_ref, ot_ref,
                 zbuf, hbuf, tbuf, wbuf, tot_ref, c_ref, acc_ref, *, head_dim, n_pairs):
    blk = u_ref.shape[0]
    q_row = lax.broadcasted_iota(jnp.int32, (LANES, blk), 0)
    c_ref[...] = jnp.zeros_like(c_ref)
    acc_ref[...] = jnp.zeros_like(acc_ref)
    heads = range(HEADS_PER_STEP)

    def scores(p, slot):
        hp, qi, kj = hp_tab[p], qi_tab[p], kj_tab[p]
        kb = k_ref[hp, pl.ds(pl.multiple_of(kj * blk, blk), blk), :]
        qt = qt_ref[hp, qi]
        bias = bias_ref[(qi == kj).astype(jnp.int32)]
        for h in heads:
            qh = jnp.where(q_row // head_dim == h, qt, jnp.zeros_like(qt))
            zbuf[slot, h] = jnp.dot(kb, qh, preferred_element_type=F32) + bias

    def softplus(p, slot):
        for h in heads:
            z = zbuf[slot, h]
            sp = jnp.maximum(z, 0.0) + jnp.log2(1.0 + jnp.exp2(-jnp.abs(z)))
            hbuf[slot, h] = sp.astype(BF16)
            zbuf[slot, h] = z - sp

    def later_key_sums(p, slot):
        for h in heads:
            r = jnp.dot(u_ref[...], hbuf[slot, h], preferred_element_type=F32)
            tbuf[slot, h] = (zbuf[slot, h] - r).astype(BF16)
            tot_ref[slot, h] = r[0:1, :] + hbuf[slot, h, 0:1, :].astype(F32)

    def weights(p, slot):
        for h in heads:
            wbuf[slot, h] = jnp.exp2(tbuf[slot, h])

    def values(p, slot):
        hp, qi, kj = hp_tab[p], qi_tab[p], kj_tab[p]
        first = qi == kj
        for h in heads:
            rows = slice(h * head_dim, (h + 1) * head_dim)
            pv = jnp.dot(vt_ref[hp, kj, rows, :], wbuf[slot, h], preferred_element_type=F32)
            c = jnp.where(first, 0.0, c_ref[h])
            acc = jnp.where(first, 0.0, acc_ref[h]) + pv * jnp.exp2(-c)
            acc_ref[h] = acc
            c_ref[h] = c + tot_ref[slot, h]
            ot_ref[hp, qi, rows, :] = acc.astype(ot_ref.dtype)

    stages = (scores, softplus, later_key_sums, weights, values)

    depth = len(stages)

    def step(p, p_mod, lo, hi):
        for k in reversed(range(lo, hi)):
            stages[k](p - k, (p_mod - k) % PIPE_SLOTS)

    main_start = PIPE_SLOTS * ((depth - 1 + PIPE_SLOTS - 1) // PIPE_SLOTS)
    main_end = n_pairs - n_pairs % PIPE_SLOTS
    assert main_start <= main_end
    for p in range(main_start):
        step(p, p % PIPE_SLOTS, 0, min(p + 1, depth))

    def body(i, carry):
        base = main_start + i * PIPE_SLOTS
        for d in range(PIPE_SLOTS):
            step(base + d, d, 0, depth)
        return carry

    lax.fori_loop(0, (main_end - main_start) // PIPE_SLOTS, body, 0)
    for p in range(main_end, n_pairs + depth - 1):
        step(p, p % PIPE_SLOTS, max(0, p - n_pairs + 1), depth)


def _attention(qt, k, vt):
    batch, n_hp, n_blk, _, blk = qt.shape
    seq = n_blk * blk
    head_dim = LANES // HEADS_PER_STEP
    pairs = [(hp, qi, kj) for hp in range(n_hp) for qi in range(n_blk) for kj in range(qi, -1, -1)]
    tabs = [jnp.asarray(np.array([p[c] for p in pairs], np.int32)) for c in range(3)]

    strictly_causal = (lax.broadcasted_iota(jnp.int32, (blk, blk), 0)
                       < lax.broadcasted_iota(jnp.int32, (blk, blk), 1))
    u = strictly_causal.astype(BF16)
    bias = jnp.stack([jnp.zeros((blk, blk), F32), jnp.where(strictly_causal, 0.0, -jnp.inf).astype(F32)])

    tspec = pl.BlockSpec((None, n_hp, n_blk, LANES, blk), lambda b, *_: (b, 0, 0, 0, 0))
    return pl.pallas_call(
        functools.partial(_attn_kernel, head_dim=head_dim, n_pairs=len(pairs)),
        grid_spec=pltpu.PrefetchScalarGridSpec(
            num_scalar_prefetch=len(tabs),
            grid=(batch,),
            in_specs=[pl.BlockSpec((None, n_hp, seq, LANES), lambda b, *_: (b, 0, 0, 0)), tspec, tspec,
                      _resident(u.shape), _resident(bias.shape)],
            out_specs=tspec,
            scratch_shapes=[
                pltpu.VMEM((PIPE_SLOTS, HEADS_PER_STEP, blk, blk), F32),
                pltpu.VMEM((PIPE_SLOTS, HEADS_PER_STEP, blk, blk), BF16),
                pltpu.VMEM((PIPE_SLOTS, HEADS_PER_STEP, blk, blk), BF16),
                pltpu.VMEM((PIPE_SLOTS, HEADS_PER_STEP, blk, blk), BF16),
                pltpu.VMEM((PIPE_SLOTS, HEADS_PER_STEP, 1, blk), F32),
                pltpu.VMEM((HEADS_PER_STEP, 1, blk), F32),
                pltpu.VMEM((HEADS_PER_STEP, head_dim, blk), F32),
            ]),
        out_shape=jax.ShapeDtypeStruct(qt.shape, BF16),
        compiler_params=_params("parallel"),
        name="stickbreaking_attention",
    )(*tabs, k, qt, vt, u, bias)


def kernel(x, ffn1_norm, ffn1_w_gate, ffn1_w_up, ffn1_w_down, mix_norm, ffn2_norm, ffn2_w_gate, ffn2_w_up, ffn2_w_down, conv_w_pw1, conv_b_pw1, conv_w_dw, conv_b_dw, conv_ln_g, conv_ln_b, conv_w_pw2, conv_b_pw2, kv_norm, w_kv, attn_w_q, attn_w_o, final_norm):
    batch, seq, d = x.shape
    depth = ffn1_norm.shape[0]
    n_conv = conv_w_pw1.shape[0]
    head_dim = d // N_HEADS
    assert head_dim * HEADS_PER_STEP == LANES

    cast = lambda w: w.astype(BF16)
    q_scale = math.log2(math.e) * head_dim ** -0.5
    row = lambda v: v.reshape(1, -1)

    ffn_stacks = {1: [ffn1_w_gate, ffn1_w_up, ffn1_w_down], 2: [ffn2_w_gate, ffn2_w_up, ffn2_w_down]}
    ffn_norms = {1: ffn1_norm, 2: ffn2_norm}
    calls = [(which, layer) for layer in range(depth) for which in (1, 2)]
    state = {"w": [cast(w[0]) for w in ffn_stacks[1]], "n": 0}
    geom = dict(batch=batch, seq=seq, blk=256)

    def ffn(h, **kw):
        which, layer = calls[state["n"]]
        state["n"] += 1
        nxt = calls[state["n"]] if state["n"] < len(calls) else None
        convert = None if nxt is None else (nxt[1], ffn_stacks[nxt[0]])
        y, posts, state["w"] = _ffn(h, row(ffn_norms[which][layer]), *state["w"], convert=convert, **kw, **geom)
        return (y, *posts) if posts else y

    h = x.reshape(batch * seq, d)
    k = vt = None
    for layer in range(depth):
        final_g = row(final_norm) if layer == depth - 1 else None
        if layer < n_conv:
            i = layer
            h = ffn(h)
            h = _conv_module(h, row(mix_norm[layer]), cast(conv_w_pw1[i]), row(conv_b_pw1[i]), conv_w_dw[i],
                             row(conv_b_dw[i]), row(conv_ln_g[i]), row(conv_ln_b[i]), cast(conv_w_pw2[i]),
                             row(conv_b_pw2[i]), seq=seq)
            if layer == n_conv - 1:
                h, k, vt = ffn(h, final_g=final_g, post_g=row(kv_norm),
                               post=((NATURAL, cast(w_kv[:, :d])), (TRANSPOSED, cast(w_kv[:, d:].T))))
            else:
                h = ffn(h, final_g=final_g)
        else:
            i = layer - n_conv
            h, qt = ffn(h, post_g=row(mix_norm[layer]), post=((TRANSPOSED, cast(attn_w_q[i].T * q_scale)),))
            h = ffn(h, pre=(_attention(qt, k, vt), cast(attn_w_o[i])), final_g=final_g)
    return h.reshape(batch, seq, d)
```

```python
import functools
import math

import numpy as np
import jax
import jax.numpy as jnp
from jax import lax
from jax.experimental import pallas as pl
from jax.experimental.pallas import tpu as pltpu

RMS_EPS = 1e-6
LN_EPS = 1e-5
HALF_STEP = 0.5
N_HEADS = 16
LANES = 128
SUBLANES = 8
CONV_HALO = 32
HEADS_PER_STEP = 2
PIPE_SLOTS = 12
VMEM_LIMIT = 56 * 1024 * 1024

F32 = jnp.float32
BF16 = jnp.bfloat16


def _rms(xf, g):
    ms = jnp.mean(xf * xf, axis=-1, keepdims=True)
    return xf * lax.rsqrt(ms + RMS_EPS) * g


def _resident(shape):
    zeros = (0,) * len(shape)
    return pl.BlockSpec(shape, lambda *_: zeros, pipeline_mode=pl.Buffered(1))


def _params(*sem):
    return pltpu.CompilerParams(dimension_semantics=sem, vmem_limit_bytes=VMEM_LIMIT)


NATURAL, TRANSPOSED = "natural", "transposed"
NT_DIMS = (((1,), (1,)), ((), ()))


def _ffn_kernel(*refs, tf, blk, pre_proj, post, final_norm, n_convert):
    refs = list(refs)
    x_ref = refs.pop(0)
    if pre_proj:
        ot_ref, wo_ref = refs.pop(0), refs.pop(0)
    g_ref, wg_ref, wu_ref, wd_ref = (refs.pop(0) for _ in range(4))
    if final_norm:
        fg_ref = refs.pop(0)
    if post:
        pg_ref = refs.pop(0)
    post_w = [refs.pop(0) for _ in post]
    cv_in = [refs.pop(0) for _ in range(n_convert)]
    o_ref = refs.pop(0)
    post_out = [refs.pop(0) for _ in post]
    cv_out = [refs.pop(0) for _ in range(n_convert)]
    hm_ref = refs.pop(0)
    for src, dst in zip(cv_in, cv_out):
        dst[...] = src[...].astype(dst.dtype)
    if pre_proj:
        xs_ref = refs.pop(0)
        n_hp = ot_ref.shape[0]
        for j in range(ot_ref.shape[1]):
            rows = slice(j * blk, (j + 1) * blk)
            o = jnp.concatenate([ot_ref[hp, j].astype(F32).T.astype(BF16) for hp in range(n_hp)], axis=1)
            xs_ref[rows, :] = x_ref[rows, :] + jnp.dot(o, wo_ref[...], preferred_element_type=F32)
        x_ref = xs_ref

    xn = _rms(x_ref[...], g_ref[...]).astype(BF16)
    for lo in range(0, hm_ref.shape[1], tf):
        g = jnp.dot(xn, wg_ref[:, lo:lo + tf], preferred_element_type=F32)
        u = jnp.dot(xn, wu_ref[:, lo:lo + tf], preferred_element_type=F32)
        hm_ref[:, lo:lo + tf] = (g * jax.nn.sigmoid(g) * u).astype(BF16)
    y = x_ref[...] + HALF_STEP * jnp.dot(hm_ref[...], wd_ref[...], preferred_element_type=F32)
    if final_norm:
        y = _rms(y, fg_ref[...])
    o_ref[...] = y

    if post:
        yn = _rms(y, pg_ref[...]).astype(BF16)
    for layout, w_ref, out_ref in zip(post, post_w, post_out):
        if layout == NATURAL:
            res = jnp.dot(yn, w_ref[...], preferred_element_type=F32)
            for hp in range(out_ref.shape[0]):
                out_ref[hp] = res[:, hp * LANES:(hp + 1) * LANES].astype(out_ref.dtype)
        else:
            res_t = lax.dot_general(w_ref[...], yn, NT_DIMS, preferred_element_type=F32)
            for hp in range(out_ref.shape[0]):
                for j in range(out_ref.shape[1]):
                    out_ref[hp, j] = res_t[hp * LANES:(hp + 1) * LANES, j * blk:(j + 1) * blk].astype(out_ref.dtype)


BF16_ROWS = 2 * SUBLANES


def _ffn(h, norm_g, wg, wu, wd, *, batch, seq, blk, pre=None, final_g=None, post_g=None, post=(),
         convert=None, tm=512, tf=256):
    m, d = h.shape
    d_ff = wg.shape[1]
    n_hp = d // LANES
    tiles = seq // tm
    steps = batch * tiles
    row = pl.BlockSpec((tm, d), lambda b, i: (b * tiles + i, 0))
    vec = _resident((1, d))
    layout_spec = {
        NATURAL: pl.BlockSpec((None, n_hp, tm, LANES), lambda b, i: (b, 0, i, 0)),
        TRANSPOSED: pl.BlockSpec((None, n_hp, tm // blk, LANES, blk), lambda b, i: (b, 0, i, 0, 0)),
    }
    layout_shape = {NATURAL: (batch, n_hp, seq, LANES), TRANSPOSED: (batch, n_hp, seq // blk, LANES, blk)}

    in_specs, args = [row], [h]
    if pre is not None:
        in_specs += [layout_spec[TRANSPOSED], _resident(pre[1].shape)]
        args += list(pre)
    in_specs += [vec, _resident(wg.shape), _resident(wu.shape), _resident(wd.shape)]
    args += [norm_g, wg, wu, wd]
    if final_g is not None:
        in_specs.append(vec)
        args.append(final_g)
    if post:
        in_specs.append(vec)
        args.append(post_g)
    in_specs += [_resident(w.shape) for _, w in post]
    args += [w for _, w in post]
    layouts = tuple(layout for layout, _ in post)
    out_specs = [row] + [layout_spec[layout] for layout in layouts]
    out_shape = ([jax.ShapeDtypeStruct((m, d), F32)]
                 + [jax.ShapeDtypeStruct(layout_shape[layout], BF16) for layout in layouts])
    cv_shapes = []
    if convert is not None:
        cv_layer, stacks = convert
        for w in stacks:
            n_rows, n_cols = w.shape[1:]
            share = next(s for s in (1, 2, 4, 8) if n_rows % (steps // s * BF16_ROWS) == 0)
            rows = n_rows // (steps // share)
            args.append(w)
            in_specs.append(pl.BlockSpec((None, rows, n_cols),
                                         lambda b, i, share=share: (cv_layer, (b * tiles + i) // share, 0)))
            out_specs.append(pl.BlockSpec((rows, n_cols), lambda b, i, share=share: ((b * tiles + i) // share, 0)))
            out_shape.append(jax.ShapeDtypeStruct((n_rows, n_cols), BF16))
            cv_shapes.append((n_rows, n_cols))
    scratch = [pltpu.VMEM((tm, d_ff), BF16)]
    if pre is not None:
        scratch.append(pltpu.VMEM((tm, d), F32))
    out = pl.pallas_call(
        functools.partial(_ffn_kernel, tf=tf, blk=blk, pre_proj=pre is not None, post=layouts,
                          final_norm=final_g is not None, n_convert=len(cv_shapes)),
        grid=(batch, tiles),
        in_specs=in_specs,
        out_specs=out_specs,
        out_shape=out_shape,
        scratch_shapes=scratch,
        compiler_params=_params("parallel", "parallel"),
        name="ffn",
    )(*args)
    n_main = 1 + len(layouts)
    converted = [o.reshape(s) for o, s in zip(out[n_main:], cv_shapes)]
    return out[0], list(out[1:n_main]), converted


def _conv_kernel(x_ref, ng_ref, w1_ref, b1_ref, wdw_ref, bdw_ref, lng_ref, lnb_ref, w2_ref, b2_ref,
                 o_ref, gbuf_ref, cbuf_ref, *, tiles_per_seq, width, col_chunk, row_chunk):
    tm, d = x_ref.shape
    i = pl.program_id(0)

    @pl.when(i % tiles_per_seq == 0)
    def _():
        gbuf_ref[0:CONV_HALO, :] = jnp.zeros((CONV_HALO, d), F32)

    @pl.when(i % tiles_per_seq != 0)
    def _():
        gbuf_ref[0:CONV_HALO, :] = gbuf_ref[tm:tm + CONV_HALO, :]

    u = _rms(x_ref[...], ng_ref[...]).astype(BF16)
    part_rows = tm // 2

    def glu(r0):
        ur = u[r0:r0 + part_rows]
        for lo in range(0, d, col_chunk):
            hi = lo + col_chunk
            a = jnp.dot(ur, w1_ref[:, lo:hi], preferred_element_type=F32) + b1_ref[:, lo:hi]
            gate = jnp.dot(ur, w1_ref[:, d + lo:d + hi], preferred_element_type=F32) + b1_ref[:, d + lo:d + hi]
            gbuf_ref[CONV_HALO + r0:CONV_HALO + r0 + part_rows, lo:hi] = a * jax.nn.sigmoid(gate)

    pad = CONV_HALO - (width - 1)

    def conv(r0):
        for lo in range(0, d, LANES):
            lanes = slice(lo, lo + LANES)
            acc = jnp.broadcast_to(bdw_ref[:, lanes], (row_chunk, LANES))
            for b in range(SUBLANES):
                rows = row_chunk if b == 0 else row_chunk + SUBLANES
                part = None
                for a in range((pad + width - 1) // SUBLANES + 1):
                    w = SUBLANES * a + b - pad
                    if 0 <= w < width:
                        first = r0 + SUBLANES * a
                        term = gbuf_ref[first:first + rows, lanes] * wdw_ref[w:w + 1, lanes]
                        part = term if part is None else part + term
                acc = acc + part[b:b + row_chunk]
            cbuf_ref[r0:r0 + row_chunk, lanes] = acc

    def project(r0):
        rows = slice(r0, r0 + part_rows)
        hc = cbuf_ref[rows, :]
        mu = jnp.mean(hc, axis=-1, keepdims=True)
        xc = hc - mu
        var = jnp.mean(xc * xc, axis=-1, keepdims=True)
        y = xc * lax.rsqrt(var + LN_EPS) * lng_ref[...] + lnb_ref[...]
        y = (y * jax.nn.sigmoid(y)).astype(BF16)
        o_ref[rows, :] = x_ref[rows, :] + jnp.dot(y, w2_ref[...], preferred_element_type=F32) + b2_ref[...]

    glu(0)
    for r0 in range(0, tm, part_rows):
        if r0 + part_rows < tm:
            glu(r0 + part_rows)
        for rc in range(r0, r0 + part_rows, row_chunk):
            conv(rc)
        project(r0)


def _conv_module(h, norm_g, w1, b1, wdw, bdw, lng, lnb, w2, b2, *, seq, tm=512):
    m, d = h.shape
    width = wdw.shape[0]
    assert width - 1 <= CONV_HALO and seq % tm == 0 and tm >= 2 * CONV_HALO
    row = pl.BlockSpec((tm, d), lambda i: (i, 0))
    vec = _resident((1, d))
    return pl.pallas_call(
        functools.partial(_conv_kernel, tiles_per_seq=seq // tm, width=width, col_chunk=256, row_chunk=128),
        grid=(m // tm,),
        in_specs=[row, vec, _resident(w1.shape), _resident(b1.shape), _resident(wdw.shape), vec, vec, vec,
                  _resident(w2.shape), vec],
        out_specs=row,
        out_shape=jax.ShapeDtypeStruct((m, d), F32),
        scratch_shapes=[pltpu.VMEM((CONV_HALO + tm, d), F32), pltpu.VMEM((tm, d), F32)],
        compiler_params=_params("arbitrary"),
        name="conv_module",
    )(h, norm_g, w1, b1, wdw, bdw, lng, lnb, w2, b2)


def _attn_kernel(hp_tab, qi_tab, kj_tab, k_ref, qt_ref, vt_ref, u_ref, bias_ref, ot_ref,
                 zbuf, hbuf, tbuf, wbuf, tot_ref, c_ref, acc_ref, *, head_dim, n_pairs):
    blk = u_ref.shape[0]
    q_row = lax.broadcasted_iota(jnp.int32, (LANES, blk), 0)
    c_ref[...] = jnp.zeros_like(c_ref)
    acc_ref[...] = jnp.zeros_like(acc_ref)
    heads = range(HEADS_PER_STEP)

    def scores(p, slot):
        hp, qi, kj = hp_tab[p], qi_tab[p], kj_tab[p]
        kb = k_ref[hp, pl.ds(pl.multiple_of(kj * blk, blk), blk), :]
        qt = qt_ref[hp, qi]
        bias = bias_ref[(qi == kj).astype(jnp.int32)]
        for h in heads:
            qh = jnp.where(q_row // head_dim == h, qt, jnp.zeros_like(qt))
            zbuf[slot, h] = jnp.dot(kb, qh, preferred_element_type=F32) + bias

    def softplus(p, slot):
        for h in heads:
            z = zbuf[slot, h]
            sp = jnp.maximum(z, 0.0) + jnp.log2(1.0 + jnp.exp2(-jnp.abs(z)))
            hbuf[slot, h] = sp.astype(BF16)
            zbuf[slot, h] = z - sp

    def later_key_sums(p, slot):
        for h in heads:
            r = jnp.dot(u_ref[...], hbuf[slot, h], preferred_element_type=F32)
            tbuf[slot, h] = (zbuf[slot, h] - r).astype(BF16)
            tot_ref[slot, h] = r[0:1, :] + hbuf[slot, h, 0:1, :].astype(F32)

    def weights(p, slot):
        for h in heads:
            wbuf[slot, h] = jnp.exp2(tbuf[slot, h])

    def values(p, slot):
        hp, qi, kj = hp_tab[p], qi_tab[p], kj_tab[p]
        first = qi == kj
        for h in heads:
            rows = slice(h * head_dim, (h + 1) * head_dim)
            pv = jnp.dot(vt_ref[hp, kj, rows, :], wbuf[slot, h], preferred_element_type=F32)
            c = jnp.where(first, 0.0, c_ref[h])
            acc = jnp.where(first, 0.0, acc_ref[h]) + pv * jnp.exp2(-c)
            acc_ref[h] = acc
            c_ref[h] = c + tot_ref[slot, h]
            ot_ref[hp, qi, rows, :] = acc.astype(ot_ref.dtype)

    stages = (scores, softplus, later_key_sums, weights, values)

    depth = len(stages)

    def step(p, p_mod, lo, hi):
        for k in reversed(range(lo, hi)):
            stages[k](p - k, (p_mod - k) % PIPE_SLOTS)

    main_start = PIPE_SLOTS * ((depth - 1 + PIPE_SLOTS - 1) // PIPE_SLOTS)
    main_end = n_pairs - n_pairs % PIPE_SLOTS
    assert main_start <= main_end
    for p in range(main_start):
        step(p, p % PIPE_SLOTS, 0, min(p + 1, depth))

    def body(i, carry):
        base = main_start + i * PIPE_SLOTS
        for d in range(PIPE_SLOTS):
            step(base + d, d, 0, depth)
        return carry

    lax.fori_loop(0, (main_end - main_start) // PIPE_SLOTS, body, 0)
    for p in range(main_end, n_pairs + depth - 1):
        step(p, p % PIPE_SLOTS, max(0, p - n_pairs + 1), depth)


def _attention(qt, k, vt):
    batch, n_hp, n_blk, _, blk = qt.shape
    seq = n_blk * blk
    head_dim = LANES // HEADS_PER_STEP
    pairs = [(hp, qi, kj) for hp in range(n_hp) for qi in range(n_blk) for kj in range(qi, -1, -1)]
    tabs = [jnp.asarray(np.array([p[c] for p in pairs], np.int32)) for c in range(3)]

    strictly_causal = (lax.broadcasted_iota(jnp.int32, (blk, blk), 0)
                       < lax.broadcasted_iota(jnp.int32, (blk, blk), 1))
    u = strictly_causal.astype(BF16)
    bias = jnp.stack([jnp.zeros((blk, blk), F32), jnp.where(strictly_causal, 0.0, -jnp.inf).astype(F32)])

    tspec = pl.BlockSpec((None, n_hp, n_blk, LANES, blk), lambda b, *_: (b, 0, 0, 0, 0))
    return pl.pallas_call(
        functools.partial(_attn_kernel, head_dim=head_dim, n_pairs=len(pairs)),
        grid_spec=pltpu.PrefetchScalarGridSpec(
            num_scalar_prefetch=len(tabs),
            grid=(batch,),
            in_specs=[pl.BlockSpec((None, n_hp, seq, LANES), lambda b, *_: (b, 0, 0, 0)), tspec, tspec,
                      _resident(u.shape), _resident(bias.shape)],
            out_specs=tspec,
            scratch_shapes=[
                pltpu.VMEM((PIPE_SLOTS, HEADS_PER_STEP, blk, blk), F32),
                pltpu.VMEM((PIPE_SLOTS, HEADS_PER_STEP, blk, blk), BF16),
                pltpu.VMEM((PIPE_SLOTS, HEADS_PER_STEP, blk, blk), BF16),
                pltpu.VMEM((PIPE_SLOTS, HEADS_PER_STEP, blk, blk), BF16),
                pltpu.VMEM((PIPE_SLOTS, HEADS_PER_STEP, 1, blk), F32),
                pltpu.VMEM((HEADS_PER_STEP, 1, blk), F32),
                pltpu.VMEM((HEADS_PER_STEP, head_dim, blk), F32),
            ]),
        out_shape=jax.ShapeDtypeStruct(qt.shape, BF16),
        compiler_params=_params("parallel"),
        name="stickbreaking_attention",
    )(*tabs, k, qt, vt, u, bias)


def kernel(x, ffn1_norm, ffn1_w_gate, ffn1_w_up, ffn1_w_down, mix_norm, ffn2_norm, ffn2_w_gate, ffn2_w_up, ffn2_w_down, conv_w_pw1, conv_b_pw1, conv_w_dw, conv_b_dw, conv_ln_g, conv_ln_b, conv_w_pw2, conv_b_pw2, kv_norm, w_kv, attn_w_q, attn_w_o, final_norm):
    batch, seq, d = x.shape
    depth = ffn1_norm.shape[0]
    n_conv = conv_w_pw1.shape[0]
    head_dim = d // N_HEADS
    assert head_dim * HEADS_PER_STEP == LANES

    cast = lambda w: w.astype(BF16)
    q_scale = math.log2(math.e) * head_dim ** -0.5
    row = lambda v: v.reshape(1, -1)

    ffn_stacks = {1: [ffn1_w_gate, ffn1_w_up, ffn1_w_down], 2: [ffn2_w_gate, ffn2_w_up, ffn2_w_down]}
    ffn_norms = {1: ffn1_norm, 2: ffn2_norm}
    calls = [(which, layer) for layer in range(depth) for which in (1, 2)]
    state = {"w": [cast(w[0]) for w in ffn_stacks[1]], "n": 0}
    geom = dict(batch=batch, seq=seq, blk=256)

    def ffn(h, **kw):
        which, layer = calls[state["n"]]
        state["n"] += 1
        nxt = calls[state["n"]] if state["n"] < len(calls) else None
        convert = None if nxt is None else (nxt[1], ffn_stacks[nxt[0]])
        y, posts, state["w"] = _ffn(h, row(ffn_norms[which][layer]), *state["w"], convert=convert, **kw, **geom)
        return (y, *posts) if posts else y

    h = x.reshape(batch * seq, d)
    k = vt = None
    for layer in range(depth):
        final_g = row(final_norm) if layer == depth - 1 else None
        if layer < n_conv:
            i = layer
            h = ffn(h)
            h = _conv_module(h, row(mix_norm[layer]), cast(conv_w_pw1[i]), row(conv_b_pw1[i]), conv_w_dw[i],
                             row(conv_b_dw[i]), row(conv_ln_g[i]), row(conv_ln_b[i]), cast(conv_w_pw2[i]),
                             row(conv_b_pw2[i]), seq=seq)
            if layer == n_conv - 1:
                h, k, vt = ffn(h, final_g=final_g, post_g=row(kv_norm),
                               post=((NATURAL, cast(w_kv[:, :d])), (TRANSPOSED, cast(w_kv[:, d:].T))))
            else:
                h = ffn(h, final_g=final_g)
        else:
            i = layer - n_conv
            h, qt = ffn(h, post_g=row(mix_norm[layer]), post=((TRANSPOSED, cast(attn_w_q[i].T * q_scale)),))
            h = ffn(h, pre=(_attention(qt, k, vt), cast(attn_w_o[i])), final_g=final_g)
    return h.reshape(batch, seq, d)
```
